```python
import math
import jax
import jax.numpy as jnp
from jax import lax
import numpy as np

D_MODEL = 1024
BATCH = 2
SEQ = 8192
DEPTH = 4

MLA_HEADS = 8
MLA_Q_RANK = 256
MLA_KV_RANK = 128
MLA_NOPE = 64
MLA_ROPE = 32
MLA_V = 64
ATTN_QBLOCK = 128
RET_HEADS = 4
RET_DK = 64
RET_DV = 128
RET_CHUNK = 128
S5_GROUP_CH = 16
S5_WIDTH = 512
S5_GROUPS = S5_WIDTH // S5_GROUP_CH
S5_STATE = 64
D_FF = ((8 * D_MODEL // 3 + 127) // 128) * 128
FFN_CONV = 3
ROPE_BASE = 10000.0
LN_EPS = 1e-5
RMS_EPS = 1e-6
GN_EPS = 1e-5
DEEPNORM_ALPHA = (2 * DEPTH) ** 0.25
DEEPNORM_BETA = (8 * DEPTH) ** -0.25
N_BRANCH = 3
IN_WIDTH = (MLA_Q_RANK + MLA_KV_RANK + MLA_ROPE + 2 * RET_HEADS * RET_DK
            + 2 * RET_HEADS * RET_DV + S5_WIDTH + N_BRANCH * D_MODEL)

kernel_name = "hybrid_mla_retention_s5_convffn_encoder"


def _in_splits():
    widths = [MLA_Q_RANK, MLA_KV_RANK, MLA_ROPE, RET_HEADS * RET_DK, RET_HEADS * RET_DK,
              RET_HEADS * RET_DV, RET_HEADS * RET_DV, S5_WIDTH]
    return [int(v) for v in np.cumsum(widths)]


def _layer_norm(x):
    xf = x.astype(jnp.float32)
    mu = jnp.mean(xf, axis=-1, keepdims=True)
    var = jnp.mean(jnp.square(xf - mu), axis=-1, keepdims=True)
    return (xf - mu) * lax.rsqrt(var + LN_EPS)


def _post_norm(x, g, b):
    return (_layer_norm(x) * g + b).astype(x.dtype)


def _rms_norm(x, g):
    xf = x.astype(jnp.float32)
    y = xf * lax.rsqrt(jnp.mean(jnp.square(xf), axis=-1, keepdims=True) + RMS_EPS)
    return (y * g).astype(x.dtype)


def _rope(x, positions):
    half = x.shape[-1] // 2
    inv_freq = ROPE_BASE ** (-jnp.arange(half, dtype=jnp.float32) / half)
    ang = positions.astype(jnp.float32)[..., None] * inv_freq
    ang = ang.reshape(ang.shape[:2] + (1,) * (x.ndim - 3) + (half,))
    cos, sin = jnp.cos(ang), jnp.sin(ang)
    x1 = x[..., :half].astype(jnp.float32)
    x2 = x[..., half:].astype(jnp.float32)
    return jnp.concatenate([x1 * cos - x2 * sin, x1 * sin + x2 * cos], axis=-1).astype(x.dtype)


def _mla(q_c, kv_c, k_rope, positions, q_norm, w_uq, kv_norm, w_ukv):
    B, S, _ = q_c.shape
    q = (_rms_norm(q_c, q_norm) @ w_uq).reshape(B, S, MLA_HEADS, MLA_NOPE + MLA_ROPE)
    q_nope = q[..., :MLA_NOPE]
    q_rope = _rope(q[..., MLA_NOPE:], positions)
    kv = (_rms_norm(kv_c, kv_norm) @ w_ukv).reshape(B, S, MLA_HEADS, MLA_NOPE + MLA_V)
    k_nope, v = kv[..., :MLA_NOPE], kv[..., MLA_NOPE:]
    k_rope = _rope(k_rope, positions)
    scale = (MLA_NOPE + MLA_ROPE) ** -0.5
    nb = S // ATTN_QBLOCK

    def to_blocks(t):
        return jnp.moveaxis(t.reshape((B, nb, ATTN_QBLOCK) + t.shape[2:]), 1, 0)

    def attend(blk):
        qn, qr = blk
        s = (jnp.einsum('bqhd,bkhd->bhqk', qn, k_nope)
             + jnp.einsum('bqhd,bkd->bhqk', qr, k_rope))
        p = jax.nn.softmax(s.astype(jnp.float32) * scale, axis=-1).astype(v.dtype)
        return jnp.einsum('bhqk,bkhd->bqhd', p, v)

    o = lax.map(attend, (to_blocks(q_nope), to_blocks(q_rope)))
    return jnp.moveaxis(o, 0, 1).reshape(B, S, MLA_HEADS * MLA_V)


def _retention_one_dir(q, k, v, log_gamma, include_diag):
    B, S, H, dk = q.shape
    dv = v.shape[-1]
    n, c = S // RET_CHUNK, RET_CHUNK
    qc = q.reshape(B, n, c, H, dk)
    kc = k.reshape(B, n, c, H, dk)
    vc = v.reshape(B, n, c, H, dv)
    idx = jnp.arange(c, dtype=jnp.float32)
    diff = idx[:, None] - idx[None, :]
    mask = (diff >= 0) if include_diag else (diff > 0)
    decay = jnp.where(mask, jnp.exp(log_gamma[:, None, None] * jnp.where(mask, diff, 0.0)), 0.0)
    scores = jnp.einsum('bnihd,bnjhd->bnhij', qc, kc) * decay
    y_intra = jnp.einsum('bnhij,bnjhe->bnihe', scores, vc)
    zeta = jnp.exp(log_gamma[:, None] * (c - 1 - idx))
    chunk_kv = jnp.einsum('bnjhd,hj,bnjhe->nbhde', kc, zeta, vc)
    carry_decay = jnp.exp(log_gamma * c)[None, :, None, None]

    def step(state, kv_n):
        return carry_decay * state + kv_n, state

    _, prev = lax.scan(step, jnp.zeros_like(chunk_kv[0]), chunk_kv)
    xi = jnp.exp(log_gamma[:, None] * (idx + 1))
    y_cross = jnp.einsum('bnihd,nbhde,hi->bnihe', qc, prev, xi)
    return (y_intra + y_cross).reshape(B, S, H, dv)


def _retention(r_q, r_k, r_v, r_g, positions, log_decay_comp):
    B, S, _ = r_q.shape
    q = _rope(r_q.reshape(B, S, RET_HEADS, RET_DK), positions) * (RET_DK ** -0.5)
    k = _rope(r_k.reshape(B, S, RET_HEADS, RET_DK), positions)
    v = r_v.reshape(B, S, RET_HEADS, RET_DV)
    log_gamma = jnp.log1p(-jnp.exp(log_decay_comp.astype(jnp.float32)))
    fwd = _retention_one_dir(q, k, v, log_gamma[0], True)
    bwd = jnp.flip(_retention_one_dir(jnp.flip(q, 1), jnp.flip(k, 1), jnp.flip(v, 1),
                                      log_gamma[1], False), 1)
    y = (fwd + bwd).astype(jnp.float32)
    mu = jnp.mean(y, axis=-1, keepdims=True)
    var = jnp.mean(jnp.square(y - mu), axis=-1, keepdims=True)
    y = ((y - mu) * lax.rsqrt(var + GN_EPS)).reshape(B, S, RET_HEADS * RET_DV)
    return (jax.nn.silu(r_g.astype(jnp.float32)) * y).astype(r_g.dtype)


def _complex_affine_combine(e1, e2):
    a1r, a1i, b1r, b1i = e1
    a2r, a2i, b2r, b2i = e2
    return (a2r * a1r - a2i * a1i,
            a2r * a1i + a2i * a1r,
            a2r * b1r - a2i * b1i + b2r,
            a2r * b1i + a2i * b1r + b2i)


def _s5_one_dir(u, lam_re, lam_im, log_step, b_re, b_im, c_re, c_im, reverse):
    step = jnp.exp(log_step)[:, None]
    mag = jnp.exp(lam_re * step)
    a_re = mag * jnp.cos(lam_im * step)
    a_im = mag * jnp.sin(lam_im * step)
    den = jnp.square(lam_re) + jnp.square(lam_im)
    f_re = ((a_re - 1.0) * lam_re + a_im * lam_im) / den
    f_im = (a_im * lam_re - (a_re - 1.0) * lam_im) / den
    bb_re = f_re[..., None] * b_re - f_im[..., None] * b_im
    bb_im = f_re[..., None] * b_im + f_im[..., None] * b_re
    bu_re = jnp.einsum('bsgc,gpc->bsgp', u, bb_re)
    bu_im = jnp.einsum('bsgc,gpc->bsgp', u, bb_im)
    elems = (jnp.broadcast_to(a_re, bu_re.shape), jnp.broadcast_to(a_im, bu_re.shape), bu_re, bu_im)
    _, _, x_re, x_im = lax.associative_scan(_complex_affine_combine, elems, reverse=reverse, axis=1)
    return jnp.einsum('bsgp,gcp->bsgc', x_re, c_re) - jnp.einsum('bsgp,gcp->bsgc', x_im, c_im)


def _s5(s5_u, lam_re, lam_im, log_step, b_re, b_im, c_re, c_im, d_skip, w_glu):
    B, S, _ = s5_u.shape
    f32 = jnp.float32
    u = s5_u.astype(f32).reshape(B, S, S5_GROUPS, S5_GROUP_CH)
    y = d_skip.astype(f32) * u
    for d in range(2):
        y = y + _s5_one_dir(u, lam_re[d].astype(f32), lam_im[d].astype(f32), log_step[d].astype(f32),
                            b_re[d].astype(f32), b_im[d].astype(f32),
                            c_re[d].astype(f32), c_im[d].astype(f32), reverse=(d == 1))
    y = jax.nn.gelu(y.reshape(B, S, S5_WIDTH)).astype(s5_u.dtype)
    return y * jax.nn.sigmoid(y @ w_glu)


def _conv_ffn(h, w_up, conv_w, conv_b, w_down):
    up = h @ w_up
    up = lax.conv_general_dilated(up, conv_w[:, None, :].astype(up.dtype), window_strides=(1,),
                                  padding='SAME', dimension_numbers=('NWC', 'WIO', 'NWC'),
                                  feature_group_count=up.shape[-1]) + conv_b
    a, g = jnp.split(up, 2, axis=-1)
    return (jax.nn.silu(g) * a) @ w_down


def setup_inputs(seed: int = 0) -> dict:
    key = jax.random.key(seed)
    ks = iter(jax.random.split(key, 48))
    L = DEPTH
    f32 = jnp.float32

    def nrm(shape, scale):
        return scale * jax.random.normal(next(ks), shape, f32)

    beta = DEEPNORM_BETA
    G, P, CH = S5_GROUPS, S5_STATE, S5_GROUP_CH
    lam_re = -0.5 * (1.0 + 0.02 * jax.random.normal(next(ks), (L, 2, G, P), f32))
    lam_im = jnp.broadcast_to(math.pi * jnp.arange(P, dtype=f32), (L, 2, G, P))
    log_step = jax.random.uniform(next(ks), (L, 2, G), f32, math.log(1e-3), math.log(1e-1))
    ret_log_decay = (-(5.0 + jnp.arange(RET_HEADS, dtype=f32)) * math.log(2.0)
                     + 0.01 * jax.random.normal(next(ks), (L, 2, RET_HEADS), f32))
    return {
        "x": nrm((BATCH, SEQ, D_MODEL), 1.0),
        "c": nrm((BATCH, D_MODEL), 1.0),
        "positions": jnp.broadcast_to(jnp.arange(SEQ, dtype=jnp.int32), (BATCH, SEQ)),
        "w_in": nrm((L, D_MODEL, IN_WIDTH), D_MODEL ** -0.5),
        "mla_q_norm": 1.0 + nrm((L, MLA_Q_RANK), 0.05),
        "mla_w_uq": nrm((L, MLA_Q_RANK, MLA_HEADS * (MLA_NOPE + MLA_ROPE)), MLA_Q_RANK ** -0.5),
        "mla_kv_norm": 1.0 + nrm((L, MLA_KV_RANK), 0.05),
        "mla_w_ukv": nrm((L, MLA_KV_RANK, MLA_HEADS * (MLA_NOPE + MLA_V)), MLA_KV_RANK ** -0.5),
        "ret_log_decay": ret_log_decay,
        "s5_lam_re": lam_re,
        "s5_lam_im": lam_im,
        "s5_log_step": log_step,
        "s5_b_re": nrm((L, 2, G, P, CH), (2 * CH) ** -0.5),
        "s5_b_im": nrm((L, 2, G, P, CH), (2 * CH) ** -0.5),
        "s5_c_re": nrm((L, 2, G, CH, P), P ** -0.5),
        "s5_c_im": nrm((L, 2, G, CH, P), P ** -0.5),
        "s5_d": nrm((L, G, CH), 1.0),
        "s5_w_glu": nrm((L, S5_WIDTH, S5_WIDTH), S5_WIDTH ** -0.5),
        "w_branch_mla": nrm((L, MLA_HEADS * MLA_V, D_MODEL), beta * (MLA_HEADS * MLA_V) ** -0.5),
        "w_branch_ret": nrm((L, RET_HEADS * RET_DV, D_MODEL), beta * (RET_HEADS * RET_DV) ** -0.5),
        "w_branch_s5": nrm((L, S5_WIDTH, D_MODEL), beta * S5_WIDTH ** -0.5),
        "w_o": nrm((L, D_MODEL, D_MODEL), beta * D_MODEL ** -0.5),
        "ffn_w_up": nrm((L, D_MODEL, 2 * D_FF), D_MODEL ** -0.5),
        "ffn_conv_w": nrm((L, FFN_CONV, 2 * D_FF), FFN_CONV ** -0.5),
        "ffn_conv_b": nrm((L, 2 * D_FF), 0.02),
        "ffn_w_down": nrm((L, D_FF, D_MODEL), beta * D_FF ** -0.5),
        "ln1_g": 1.0 + nrm((L, D_MODEL), 0.05),
        "ln1_b": nrm((L, D_MODEL), 0.02),
        "ln2_g": 1.0 + nrm((L, D_MODEL), 0.05),
        "ln2_b": nrm((L, D_MODEL), 0.02),
        "w_ada": nrm((L, D_MODEL, 6 * D_MODEL), D_MODEL ** -0.5),
        "b_ada": nrm((L, 6 * D_MODEL), 0.02),
    }


def reference(x, c, positions, w_in, mla_q_norm, mla_w_uq, mla_kv_norm, mla_w_ukv,
              ret_log_decay, s5_lam_re, s5_lam_im, s5_log_step, s5_b_re, s5_b_im,
              s5_c_re, s5_c_im, s5_d, s5_w_glu, w_branch_mla, w_branch_ret, w_branch_s5,
              w_o, ffn_w_up, ffn_conv_w, ffn_conv_b, ffn_w_down, ln1_g, ln1_b, ln2_g, ln2_b,
              w_ada, b_ada):
    splits = _in_splits()
    cond = jax.nn.silu(c)
    for l in range(DEPTH):
        mod = cond @ w_ada[l] + b_ada[l]
        shift1, scale1, gate1, shift2, scale2, gate2 = [m[:, None, :] for m in jnp.split(mod, 6, axis=-1)]

        h = (_layer_norm(x) * (1.0 + scale1) + shift1).astype(x.dtype)
        proj = h @ w_in[l]
        q_c, kv_c, k_r, r_q, r_k, r_v, r_g, s5_u, gates = jnp.split(proj, splits, axis=-1)
        o_mla = _mla(q_c, kv_c, k_r, positions, mla_q_norm[l], mla_w_uq[l], mla_kv_norm[l], mla_w_ukv[l])
        o_ret = _retention(r_q, r_k, r_v, r_g, positions, ret_log_decay[l])
        o_s5 = _s5(s5_u, s5_lam_re[l], s5_lam_im[l], s5_log_step[l], s5_b_re[l], s5_b_im[l],
                   s5_c_re[l], s5_c_im[l], s5_d[l], s5_w_glu[l])
        g_mla, g_ret, g_s5 = jnp.split(jax.nn.sigmoid(gates), N_BRANCH, axis=-1)
        merged = (g_mla * (o_mla @ w_branch_mla[l])
                  + g_ret * (o_ret @ w_branch_ret[l])
                  + g_s5 * (o_s5 @ w_branch_s5[l]))
        x = _post_norm(DEEPNORM_ALPHA * x + gate1 * (merged @ w_o[l]), ln1_g[l], ln1_b[l])

        h = (_layer_norm(x) * (1.0 + scale2) + shift2).astype(x.dtype)
        f = _conv_ffn(h, ffn_w_up[l], ffn_conv_w[l], ffn_conv_b[l], ffn_w_down[l])
        x = _post_norm(DEEPNORM_ALPHA * x + gate2 * f, ln2_g[l], ln2_b[l])
    return x
```

```python
import functools
import math

import jax
import jax.numpy as jnp
import numpy as np
from jax import lax
from jax.experimental import pallas as pl
from jax.experimental.pallas import tpu as pltpu

F32 = jnp.float32
BF16 = jnp.bfloat16
HI = lax.Precision.HIGHEST

MLA_HEADS = 8
MLA_Q_RANK = 256
MLA_KV_RANK = 128
MLA_NOPE = 64
MLA_ROPE = 32
MLA_V = 64
RET_HEADS = 4
RET_DK = 64
RET_DV = 128
RET_CHUNK = 128
S5_GROUP_CH = 16
S5_WIDTH = 512
S5_GROUPS = S5_WIDTH // S5_GROUP_CH
S5_STATE = 64
S5_CHUNK = 16
S5_PAIRS = S5_GROUPS // 2
ROPE_BASE = 10000.0
LN_EPS = 1e-5
RMS_EPS = 1e-6
GN_EPS = 1e-5
FFN_CHUNK = 256

LANES = 128
HEAD_PAD = 128
VMEM_LIMIT = 56 * 1024 * 1024


def _cparams(n_axes):
    return pltpu.CompilerParams(dimension_semantics=("arbitrary",) * n_axes,
                                vmem_limit_bytes=VMEM_LIMIT)


def _const_spec(shape):
    nd = len(shape)
    return pl.BlockSpec(shape, lambda *_: (0,) * nd)


def _layer_norm(x):
    mu = jnp.mean(x, axis=-1, keepdims=True)
    xc = x - mu
    var = jnp.mean(xc * xc, axis=-1, keepdims=True)
    return xc * lax.rsqrt(var + LN_EPS)


def _rms_norm(x, g):
    return x * lax.rsqrt(jnp.mean(x * x, axis=-1, keepdims=True) + RMS_EPS) * g


def _dot(a, b):
    return jnp.dot(a, b, preferred_element_type=F32)


def _dot_nt(a, b):
    return lax.dot_general(a, b, (((1,), (1,)), ((), ())), preferred_element_type=F32)


def _dot_tn(a, b):
    return lax.dot_general(a, b, (((0,), (0,)), ((), ())), preferred_element_type=F32)


def _rope_body(pos_ref, invf_ref, cm_ref, sm_ref, cr_ref, sr_ref):
    p = pos_ref[...].astype(F32)
    ang_m = p * invf_ref[0:1, :]
    ang_r = p * invf_ref[1:2, :]
    cm_ref[...] = jnp.cos(ang_m)
    sm_ref[...] = jnp.sin(ang_m)
    cr_ref[...] = jnp.cos(ang_r)
    sr_ref[...] = jnp.sin(ang_r)


def _rope_tables(positions, tm):
    T = positions.size
    half_m, half_r = MLA_ROPE // 2, RET_DK // 2
    inv_m = ROPE_BASE ** (-jnp.arange(half_m, dtype=F32) / half_m)
    inv_r = ROPE_BASE ** (-jnp.arange(half_r, dtype=F32) / half_r)
    lane = np.arange(LANES)
    row_m = jnp.where((lane >= MLA_NOPE) & (lane < MLA_NOPE + MLA_ROPE),
                      inv_m[(lane - MLA_NOPE) % half_m], 0.0)
    row_r = inv_r[lane % half_r]
    invf = jnp.stack([row_m, row_r]).astype(F32)
    out = jax.ShapeDtypeStruct((T, LANES), F32)
    spec = pl.BlockSpec((tm, LANES), lambda i: (i, 0))
    return pl.pallas_call(
        _rope_body, grid=(T // tm,),
        in_specs=[pl.BlockSpec((tm, 1), lambda i: (i, 0)), _const_spec((2, LANES))],
        out_specs=[spec] * 4, out_shape=[out] * 4,
        compiler_params=_cparams(1), name="rope_tables",
    )(positions.reshape(T, 1), invf)


def _mod_body(c_ref, w_ref, b_ref, o_ref):
    cond = jax.nn.silu(c_ref[...]).astype(BF16)
    o_ref[0] = _dot(cond, w_ref[0].astype(BF16)) + b_ref[0]


def _ada_mod(c, w_ada, b_ada):
    L, D, N = w_ada.shape
    B = c.shape[0]
    tn = D
    return pl.pallas_call(
        _mod_body, grid=(L, N // tn),
        in_specs=[_const_spec((B, D)),
                  pl.BlockSpec((1, D, tn), lambda l, j: (l, 0, j)),
                  pl.BlockSpec((1, 1, tn), lambda l, j: (l, 0, j))],
        out_specs=pl.BlockSpec((1, B, tn), lambda l, j: (l, 0, j)),
        out_shape=jax.ShapeDtypeStruct((L, B, N), F32),
        compiler_params=_cparams(2), name="ada_mod",
    )(c, w_ada, b_ada.reshape(L, 1, N))


_IN_BLOCKS = (("qc", 256), ("kvc", 128), ("kr", 256), ("rq", 512), ("rk", 512),
              ("rv", 512), ("rg", 512), ("u", 512), ("gates", 3072))
_IN_WIDTH = sum(w for _, w in _IN_BLOCKS)


def _in_proj_body(x_ref, mod_ref, w_ref, *out_refs):
    D = x_ref.shape[1]
    xn = _layer_norm(x_ref[...])
    shift = mod_ref[0, :, 0:D]
    scale = mod_ref[0, :, D:2 * D]
    h = (xn * (1.0 + scale) + shift).astype(BF16)
    lo = 0
    for (_, width), o_ref in zip(_IN_BLOCKS, out_refs):
        for c0 in range(0, width, 512):
            c1 = min(c0 + 512, width)
            o_ref[:, c0:c1] = _dot(h, w_ref[:, lo + c0:lo + c1]).astype(o_ref.dtype)
        lo += width


def _in_proj(x, mod_l, w_packed, tm, tiles_per_seq):
    T, D = x.shape
    row = lambda i: (i, 0)
    return pl.pallas_call(
        _in_proj_body, grid=(T // tm,),
        in_specs=[pl.BlockSpec((tm, D), row),
                  pl.BlockSpec((1, 1, mod_l.shape[2]), lambda i: (i // tiles_per_seq, 0, 0)),
                  _const_spec(w_packed.shape)],
        out_specs=[pl.BlockSpec((tm, w), row) for _, w in _IN_BLOCKS],
        out_shape=[jax.ShapeDtypeStruct((T, w), BF16) for _, w in _IN_BLOCKS],
        compiler_params=_cparams(1), name="in_proj",
    )(x, mod_l, w_packed)


def _mla_prep_body(qc_ref, kvc_ref, kr_ref, cos_ref, sin_ref, qg_ref, kvg_ref,
                   wq_ref, wqr_ref, wk_ref, wv_ref, q_ref, k_ref, v_ref):
    cos, sin = cos_ref[...], sin_ref[...]
    cos8 = jnp.concatenate([cos] * MLA_HEADS, axis=1)
    sin8 = jnp.concatenate([sin] * MLA_HEADS, axis=1)
    qn = _rms_norm(qc_ref[...].astype(F32), qg_ref[...]).astype(BF16)
    scale = (MLA_NOPE + MLA_ROPE) ** -0.5
    q = (_dot(qn, wq_ref[...]) * cos8 + _dot(qn, wqr_ref[...]) * sin8) * scale
    q_ref[...] = q.astype(BF16)
    kvn = _rms_norm(kvc_ref[...].astype(F32), kvg_ref[...]).astype(BF16)
    kr = kr_ref[...].astype(F32)
    k_rope = kr[:, :HEAD_PAD] * cos + kr[:, HEAD_PAD:] * sin
    k = _dot(kvn, wk_ref[...]) + jnp.concatenate([k_rope] * MLA_HEADS, axis=1)
    k_ref[...] = k.astype(BF16)
    v_ref[...] = _dot(kvn, wv_ref[...]).astype(BF16)


def _mla_prep(qc, kvc, kr, cos_m, sin_m, q_gain, kv_gain, wq, wqr, wk, wv, tm):
    T = qc.shape[0]
    row = lambda i: (i, 0)
    qk_w = MLA_HEADS * HEAD_PAD
    v_w = MLA_HEADS * MLA_V
    return pl.pallas_call(
        _mla_prep_body, grid=(T // tm,),
        in_specs=[pl.BlockSpec((tm, qc.shape[1]), row), pl.BlockSpec((tm, kvc.shape[1]), row),
                  pl.BlockSpec((tm, kr.shape[1]), row),
                  pl.BlockSpec((tm, LANES), row), pl.BlockSpec((tm, LANES), row),
                  _const_spec(q_gain.shape), _const_spec(kv_gain.shape),
                  _const_spec(wq.shape), _const_spec(wqr.shape),
                  _const_spec(wk.shape), _const_spec(wv.shape)],
        out_specs=[pl.BlockSpec((tm, qk_w), row), pl.BlockSpec((tm, qk_w), row),
                   pl.BlockSpec((tm, v_w), row)],
        out_shape=[jax.ShapeDtypeStruct((T, qk_w), BF16), jax.ShapeDtypeStruct((T, qk_w), BF16),
                   jax.ShapeDtypeStruct((T, v_w), BF16)],
        compiler_params=_cparams(1), name="mla_prep",
    )(qc, kvc, kr, cos_m, sin_m, q_gain, kv_gain, wq, wqr, wk, wv)


def _attn_body(q_ref, k_ref, v_ref, o_ref, m_ref, l_ref, acc_ref, *, tk):
    tq = q_ref.shape[0]
    nk = k_ref.shape[0] // tk
    outs = []
    for j in range(2):
        cols = slice(j * HEAD_PAD, (j + 1) * HEAD_PAD)
        q = q_ref[:, cols]
        m_ref[...] = jnp.full(m_ref.shape, -jnp.inf, F32)
        l_ref[...] = jnp.zeros(l_ref.shape, F32)
        acc_ref[...] = jnp.zeros(acc_ref.shape, F32)

        def kstep(t, carry, q=q, cols=cols):
            r0 = pl.multiple_of(t * tk, tk)
            k = k_ref[pl.ds(r0, tk), cols]
            v = v_ref[pl.ds(r0, tk), :]
            s = _dot_nt(q, k)
            m_prev = m_ref[...]
            m_new = jnp.maximum(m_prev, jnp.max(s, axis=-1, keepdims=True))
            p = jnp.exp(s - m_new)
            alpha = jnp.exp(m_prev - m_new)
            l_ref[...] = alpha * l_ref[...] + jnp.sum(p, axis=-1, keepdims=True)
            acc_ref[...] = alpha * acc_ref[...] + _dot(p.astype(BF16), v)
            m_ref[...] = m_new
            return carry

        lax.fori_loop(0, nk, kstep, 0)
        outs.append(acc_ref[...] / l_ref[...])
    lane = lax.broadcasted_iota(jnp.int32, (tq, 2 * MLA_V), 1)
    o_ref[...] = jnp.where(lane < MLA_V, outs[0], outs[1]).astype(o_ref.dtype)


def _mla_attn(q, k, v, B, S, tq, tk):
    T = q.shape[0]
    nq = S // tq
    pairs = MLA_HEADS // 2
    return pl.pallas_call(
        functools.partial(_attn_body, tk=tk), grid=(B, pairs, nq),
        in_specs=[pl.BlockSpec((tq, 2 * HEAD_PAD), lambda b, h, i: (b * nq + i, h)),
                  pl.BlockSpec((S, 2 * HEAD_PAD), lambda b, h, i: (b, h)),
                  pl.BlockSpec((S, 2 * MLA_V), lambda b, h, i: (b, h))],
        out_specs=pl.BlockSpec((tq, 2 * MLA_V), lambda b, h, i: (b * nq + i, h)),
        out_shape=jax.ShapeDtypeStruct((T, MLA_HEADS * MLA_V), BF16),
        scratch_shapes=[pltpu.VMEM((tq, 1), F32), pltpu.VMEM((tq, 1), F32),
                        pltpu.VMEM((tq, 2 * MLA_V), F32)],
        compiler_params=_cparams(3), name="mla_attn",
    )(q, k, v)


_XI_F, _ZETA_F, _XI_B, _ZETA_B, _CARRY_F, _CARRY_B = range(6)


def _ret_rope(r_ref, cos2, sin2):
    r = r_ref[...].astype(F32)
    half = r.shape[1] // 2
    return r[:, :half] * cos2 + r[:, half:] * sin2


def _head_masked(x_pair, h):
    lane = lax.broadcasted_iota(jnp.int32, x_pair.shape, 1)
    lo = (h % 2) * RET_DK
    return jnp.where((lane >= lo) & (lane < lo + RET_DK), x_pair, 0.0)


def _ret_fwd_body(rq_ref, rk_ref, rv_ref, cos_ref, sin_ref, dm_ref, vec_ref, y_ref, s_ref):
    @pl.when(pl.program_id(1) == 0)
    def _():
        s_ref[...] = jnp.zeros(s_ref.shape, F32)

    c = RET_CHUNK
    cos2 = jnp.concatenate([cos_ref[...]] * 2, axis=1)
    sin2 = jnp.concatenate([sin_ref[...]] * 2, axis=1)
    q = _ret_rope(rq_ref, cos2, sin2)
    k = _ret_rope(rk_ref, cos2, sin2)
    for ci in range(rq_ref.shape[0] // c):
        rows = slice(ci * c, (ci + 1) * c)
        for h in range(RET_HEADS):
            pair = slice((h // 2) * LANES, (h // 2 + 1) * LANES)
            vcols = slice(h * RET_DV, (h + 1) * RET_DV)
            vec = vec_ref[h]
            qm = _head_masked(q[rows, pair], h)
            kp = k[rows, pair]
            vh = rv_ref[rows, vcols]
            scores = _dot_nt(qm.astype(BF16), kp.astype(BF16)) * dm_ref[h]
            y = _dot(scores.astype(BF16), vh)
            state = s_ref[h]
            y = y + _dot((qm * vec[:, _XI_F:_XI_F + 1]).astype(BF16), state.astype(BF16))
            y_ref[rows, vcols] = y
            kz = (kp * vec[:, _ZETA_F:_ZETA_F + 1]).astype(BF16)
            s_ref[h] = vec[0:1, _CARRY_F:_CARRY_F + 1] * state + _dot_tn(kz, vh)


def _ret_bwd_body(rq_ref, rk_ref, rv_ref, rg_ref, ya_ref, cos_ref, sin_ref, vec_ref, o_ref, s_ref):
    @pl.when(pl.program_id(1) == 0)
    def _():
        s_ref[...] = jnp.zeros(s_ref.shape, F32)

    c = RET_CHUNK
    cos2 = jnp.concatenate([cos_ref[...]] * 2, axis=1)
    sin2 = jnp.concatenate([sin_ref[...]] * 2, axis=1)
    q = _ret_rope(rq_ref, cos2, sin2)
    k = _ret_rope(rk_ref, cos2, sin2)
    for ci in reversed(range(rq_ref.shape[0] // c)):
        rows = slice(ci * c, (ci + 1) * c)
        for h in range(RET_HEADS):
            pair = slice((h // 2) * LANES, (h // 2 + 1) * LANES)
            vcols = slice(h * RET_DV, (h + 1) * RET_DV)
            vec = vec_ref[h]
            qm = _head_masked(q[rows, pair], h)
            kp = k[rows, pair]
            vh = rv_ref[rows, vcols]
            state = s_ref[h]
            y = ya_ref[rows, vcols] + _dot((qm * vec[:, _XI_B:_XI_B + 1]).astype(BF16),
                                           state.astype(BF16))
            kz = (kp * vec[:, _ZETA_B:_ZETA_B + 1]).astype(BF16)
            s_ref[h] = vec[0:1, _CARRY_B:_CARRY_B + 1] * state + _dot_tn(kz, vh)
            mu = jnp.mean(y, axis=-1, keepdims=True)
            yc = y - mu
            var = jnp.mean(yc * yc, axis=-1, keepdims=True)
            yn = yc * lax.rsqrt(var + GN_EPS)
            gate = jax.nn.silu(rg_ref[rows, vcols].astype(F32))
            o_ref[rows, vcols] = (gate * yn).astype(o_ref.dtype)


def _retention(rq, rk, rv, rg, cos_r, sin_r, dmask, vec, B, S, tr):
    T = rq.shape[0]
    nb = S // tr
    w = RET_HEADS * RET_DV
    fwd = lambda b, j: (b * nb + j, 0)
    bwd = lambda b, j: (b * nb + nb - 1 - j, 0)
    state = pltpu.VMEM((RET_HEADS, LANES, RET_DV), F32)
    ya = pl.pallas_call(
        _ret_fwd_body, grid=(B, nb),
        in_specs=[pl.BlockSpec((tr, rq.shape[1]), fwd), pl.BlockSpec((tr, rk.shape[1]), fwd),
                  pl.BlockSpec((tr, w), fwd), pl.BlockSpec((tr, LANES), fwd),
                  pl.BlockSpec((tr, LANES), fwd), _const_spec(dmask.shape), _const_spec(vec.shape)],
        out_specs=pl.BlockSpec((tr, w), fwd),
        out_shape=jax.ShapeDtypeStruct((T, w), F32),
        scratch_shapes=[state], compiler_params=_cparams(2), name="ret_fwd",
    )(rq, rk, rv, cos_r, sin_r, dmask, vec)
    return pl.pallas_call(
        _ret_bwd_body, grid=(B, nb),
        in_specs=[pl.BlockSpec((tr, rq.shape[1]), bwd), pl.BlockSpec((tr, rk.shape[1]), bwd),
                  pl.BlockSpec((tr, w), bwd), pl.BlockSpec((tr, w), bwd), pl.BlockSpec((tr, w), bwd),
                  pl.BlockSpec((tr, LANES), bwd), pl.BlockSpec((tr, LANES), bwd),
                  _const_spec(vec.shape)],
        out_specs=pl.BlockSpec((tr, w), bwd),
        out_shape=jax.ShapeDtypeStruct((T, w), BF16),
        scratch_shapes=[state], compiler_params=_cparams(2), name="ret_bwd",
    )(rq, rk, rv, rg, ya, cos_r, sin_r, vec)


def _s5_in_body(u_ref, min_ref, fr_ref, fi_ref, br_ref, bi_ref):
    v = _dot(u_ref[0, 0], min_ref[0])
    for part, ref in enumerate((fr_ref, fi_ref, br_ref, bi_ref)):
        ref[0] = v[:, part * LANES:(part + 1) * LANES]


def _s5_scan_body(fr_ref, fi_ref, br_ref, bi_ref, a_ref, zfr_ref, zfi_ref, zbr_ref, zbi_ref, *, rb):
    n = fr_ref.shape[1]
    nblk = n // rb
    lw = fr_ref.shape[2]
    afr, afi = a_ref[0:1, :], a_ref[1:2, :]
    abr, abi = a_ref[2:3, :], a_ref[3:4, :]
    zero = jnp.zeros((1, lw), F32)

    def step(blk, carry):
        xr, xi, yr, yi = carry
        r0 = pl.multiple_of(blk * rb, rb)
        vr, vi = fr_ref[0, pl.ds(r0, rb), :], fi_ref[0, pl.ds(r0, rb), :]
        out_r, out_i = [], []
        for r in range(rb):
            out_r.append(xr)
            out_i.append(xi)
            xr, xi = (afr * xr - afi * xi + vr[r:r + 1], afr * xi + afi * xr + vi[r:r + 1])
        zfr_ref[0, pl.ds(r0, rb), :] = jnp.concatenate(out_r, axis=0).astype(zfr_ref.dtype)
        zfi_ref[0, pl.ds(r0, rb), :] = jnp.concatenate(out_i, axis=0).astype(zfi_ref.dtype)
        r1 = pl.multiple_of((nblk - 1 - blk) * rb, rb)
        wr, wi = br_ref[0, pl.ds(r1, rb), :], bi_ref[0, pl.ds(r1, rb), :]
        out_r, out_i = [None] * rb, [None] * rb
        for r in reversed(range(rb)):
            out_r[r] = yr
            out_i[r] = yi
            yr, yi = (abr * yr - abi * yi + wr[r:r + 1], abr * yi + abi * yr + wi[r:r + 1])
        zbr_ref[0, pl.ds(r1, rb), :] = jnp.concatenate(out_r, axis=0).astype(zbr_ref.dtype)
        zbi_ref[0, pl.ds(r1, rb), :] = jnp.concatenate(out_i, axis=0).astype(zbi_ref.dtype)
        return xr, xi, yr, yi

    lax.fori_loop(0, nblk, step, (zero, zero, zero, zero))


def _s5_out_body(u_ref, zfr_ref, zfi_ref, zbr_ref, zbi_ref, t_ref, mout_ref, y_ref):
    z = jnp.concatenate([zfr_ref[0], zfi_ref[0], zbr_ref[0], zbi_ref[0]], axis=1)
    y_ref[0, 0] = (_dot(u_ref[0, 0], t_ref[0]) + _dot(z, mout_ref[0])).astype(y_ref.dtype)


def _s5_core(u_pairs, m_in, t_mat, m_out, a_vec):
    B, P, N, W = u_pairs.shape
    sw = S5_GROUPS * S5_STATE
    u_spec = pl.BlockSpec((1, 1, N, W), lambda b, p: (b, p, 0, 0))
    part_spec = pl.BlockSpec((1, N, LANES), lambda b, p: (b, 0, p))
    mat_spec = pl.BlockSpec((1, W, W), lambda b, p: (p, 0, 0))
    parts = pl.pallas_call(
        _s5_in_body, grid=(B, P),
        in_specs=[u_spec, mat_spec], out_specs=[part_spec] * 4,
        out_shape=[jax.ShapeDtypeStruct((B, N, sw), F32)] * 4,
        compiler_params=_cparams(2), name="s5_in",
    )(u_pairs, m_in)
    lw = 512
    blk = pl.BlockSpec((1, N, lw), lambda b, j: (b, 0, j))
    z = pl.pallas_call(
        functools.partial(_s5_scan_body, rb=16), grid=(B, sw // lw),
        in_specs=[blk] * 4 + [pl.BlockSpec((4, lw), lambda b, j: (0, j))],
        out_specs=[blk] * 4, out_shape=[jax.ShapeDtypeStruct((B, N, sw), BF16)] * 4,
        compiler_params=_cparams(2), name="s5_scan",
    )(*parts, a_vec)
    return pl.pallas_call(
        _s5_out_body, grid=(B, P),
        in_specs=[u_spec] + [part_spec] * 4 + [mat_spec, mat_spec],
        out_specs=u_spec, out_shape=jax.ShapeDtypeStruct((B, P, N, W), BF16),
        compiler_params=_cparams(2), name="s5_out",
    )(u_pairs, *z, t_mat, m_out)


def _to_pairs(u, B, S):
    n = S // S5_CHUNK
    u6 = u.reshape(B, n, S5_CHUNK, S5_PAIRS, 2, S5_GROUP_CH)
    return u6.transpose(0, 3, 1, 4, 2, 5).reshape(B, S5_PAIRS, n, 2 * S5_CHUNK * S5_GROUP_CH)


def _from_pairs(y, B, S):
    n = S // S5_CHUNK
    y6 = y.reshape(B, S5_PAIRS, n, 2, S5_CHUNK, S5_GROUP_CH)
    return y6.transpose(0, 2, 4, 1, 3, 5).reshape(B * S, S5_WIDTH)


def _merge_body(x_ref, mod_ref, om_ref, or_ref, ys_ref, u_ref, g_ref, d_ref, wglu_ref,
                wm_ref, wr_ref, ws_ref, wo_ref, lng_ref, lnb_ref, o_ref, *, alpha):
    D = x_ref.shape[1]
    y = d_ref[...] * u_ref[...].astype(F32) + ys_ref[...].astype(F32)
    y = jax.nn.gelu(y)
    o_s5 = y * jax.nn.sigmoid(_dot(y.astype(BF16), wglu_ref[...]))
    merged = (jax.nn.sigmoid(g_ref[:, 0:D].astype(F32)) * _dot(om_ref[...], wm_ref[...])
              + jax.nn.sigmoid(g_ref[:, D:2 * D].astype(F32)) * _dot(or_ref[...], wr_ref[...])
              + jax.nn.sigmoid(g_ref[:, 2 * D:3 * D].astype(F32)) * _dot(o_s5.astype(BF16), ws_ref[...]))
    out = _dot(merged.astype(BF16), wo_ref[...])
    gate1 = mod_ref[0, :, 2 * D:3 * D]
    o_ref[...] = _layer_norm(alpha * x_ref[...] + gate1 * out) * lng_ref[...] + lnb_ref[...]


def _merge(x, mod_l, o_mla, o_ret, y_ssm, u, gates, d_skip, w_glu, wm, wr, ws, wo, ln_g, ln_b,
           tm, tiles_per_seq, alpha):
    T, D = x.shape
    row = lambda i: (i, 0)
    consts = (d_skip, w_glu, wm, wr, ws, wo, ln_g, ln_b)
    return pl.pallas_call(
        functools.partial(_merge_body, alpha=alpha), grid=(T // tm,),
        in_specs=[pl.BlockSpec((tm, D), row),
                  pl.BlockSpec((1, 1, mod_l.shape[2]), lambda i: (i // tiles_per_seq, 0, 0))]
                 + [pl.BlockSpec((tm, a.shape[1]), row) for a in (o_mla, o_ret, y_ssm, u, gates)]
                 + [_const_spec(a.shape) for a in consts],
        out_specs=pl.BlockSpec((tm, D), row),
        out_shape=jax.ShapeDtypeStruct((T, D), F32),
        compiler_params=_cparams(1), name="merge",
    )(x, mod_l, o_mla, o_ret, y_ssm, u, gates, *consts)


HALO = 8


def _ffn_body(x_ref, xp_ref, xn_ref, mod_ref, wa_ref, wg_ref, cwa_ref, cwg_ref, wd_ref,
              lng_ref, lnb_ref, o_ref, acc_ref, *, alpha, tiles_per_seq):
    tm, D = x_ref.shape
    i = pl.program_id(0)
    shift = mod_ref[0, :, 3 * D:4 * D]
    scale = mod_ref[0, :, 4 * D:5 * D]
    gate2 = mod_ref[0, :, 5 * D:6 * D]
    x = x_ref[...]
    x_ext = jnp.concatenate([xp_ref[...], x, xn_ref[...]], axis=0)
    h = _layer_norm(x_ext) * (1.0 + scale) + shift
    r = lax.broadcasted_iota(jnp.int32, (tm + 2 * HALO, 1), 0)
    pos_in_seq = i % tiles_per_seq
    lo = jnp.where(pos_in_seq == 0, HALO, 0)
    hi = jnp.where(pos_in_seq == tiles_per_seq - 1, tm + HALO, tm + 2 * HALO)
    h = jnp.where((r >= lo) & (r < hi), h, 0.0).astype(BF16)
    acc_ref[...] = jnp.zeros(acc_ref.shape, F32)
    n_ext = tm + 2 * HALO

    def conv(up, cw):
        prev = pltpu.roll(up, 1, 0)
        nxt = pltpu.roll(up, n_ext - 1, 0)
        y = prev * cw[0:1, :] + up * cw[1:2, :] + nxt * cw[2:3, :] + cw[3:4, :]
        return y[HALO:HALO + tm, :]

    def chunk(j, carry):
        a = conv(_dot(h, wa_ref[j]), cwa_ref[j])
        g = conv(_dot(h, wg_ref[j]), cwg_ref[j])
        act = (jax.nn.silu(g) * a).astype(BF16)
        acc_ref[...] += _dot(act, wd_ref[j])
        return carry

    lax.fori_loop(0, wa_ref.shape[0], chunk, 0)
    o_ref[...] = _layer_norm(alpha * x + gate2 * acc_ref[...]) * lng_ref[...] + lnb_ref[...]


def _ffn(x, mod_l, wa, wg, cwa, cwg, wd, ln_g, ln_b, tm, tiles_per_seq, alpha):
    T, D = x.shape
    hb = tm // HALO
    n_hb = T // HALO
    consts = (wa, wg, cwa, cwg, wd, ln_g, ln_b)
    return pl.pallas_call(
        functools.partial(_ffn_body, alpha=alpha, tiles_per_seq=tiles_per_seq), grid=(T // tm,),
        in_specs=[pl.BlockSpec((tm, D), lambda i: (i, 0)),
                  pl.BlockSpec((HALO, D), lambda i: (jnp.maximum(i * hb - 1, 0), 0)),
                  pl.BlockSpec((HALO, D), lambda i: (jnp.minimum((i + 1) * hb, n_hb - 1), 0)),
                  pl.BlockSpec((1, 1, mod_l.shape[2]), lambda i: (i // tiles_per_seq, 0, 0))]
                 + [_const_spec(a.shape) for a in consts],
        out_specs=pl.BlockSpec((tm, D), lambda i: (i, 0)),
        out_shape=jax.ShapeDtypeStruct((T, D), F32),
        scratch_shapes=[pltpu.VMEM((tm, D), F32)],
        compiler_params=_cparams(1), name="ffn",
    )(x, x, x, mod_l, *consts)


def _rot_cols(w, n_heads, dh):
    lead = w.shape[:-1]
    w4 = w.reshape(lead + (n_heads, 2, dh // 2))
    return jnp.stack([-w4[..., 1, :], w4[..., 0, :]], axis=-2).reshape(lead + (n_heads * dh,))


def _pack_w_in(w_in):
    L, D, _ = w_in.shape
    widths = [MLA_Q_RANK, MLA_KV_RANK, MLA_ROPE, RET_HEADS * RET_DK, RET_HEADS * RET_DK,
              RET_HEADS * RET_DV, RET_HEADS * RET_DV, S5_WIDTH]
    sp = [0] + [int(v) for v in np.cumsum(widths)]
    w_qc, w_kvc, w_kr, w_rq, w_rk, w_rv, w_rg, w_u = [w_in[..., sp[i]:sp[i + 1]] for i in range(8)]
    w_gates = w_in[..., sp[8]:]

    def kr_block(w):
        z = jnp.zeros((L, D, HEAD_PAD), w_in.dtype)
        return z.at[..., MLA_NOPE:MLA_NOPE + MLA_ROPE].set(w)

    w_rq = w_rq * (RET_DK ** -0.5)
    blocks = [w_qc, w_kvc, kr_block(w_kr), kr_block(_rot_cols(w_kr, 1, MLA_ROPE)),
              w_rq, _rot_cols(w_rq, RET_HEADS, RET_DK), w_rk, _rot_cols(w_rk, RET_HEADS, RET_DK),
              w_rv, w_rg, w_u, w_gates]
    return jnp.concatenate(blocks, axis=-1).astype(BF16)


def _pack_mla(w_uq, w_ukv):
    L = w_uq.shape[0]
    dq = MLA_NOPE + MLA_ROPE
    wq = w_uq.reshape(L, MLA_Q_RANK, MLA_HEADS, dq)
    pad = jnp.zeros((L, MLA_Q_RANK, MLA_HEADS, HEAD_PAD - dq), w_uq.dtype)
    wq_p = jnp.concatenate([wq, pad], axis=-1).reshape(L, MLA_Q_RANK, MLA_HEADS * HEAD_PAD)
    rot = _rot_cols(wq[..., MLA_NOPE:].reshape(L, MLA_Q_RANK, MLA_HEADS * MLA_ROPE), MLA_HEADS, MLA_ROPE)
    rot = rot.reshape(L, MLA_Q_RANK, MLA_HEADS, MLA_ROPE)
    zn = jnp.zeros((L, MLA_Q_RANK, MLA_HEADS, MLA_NOPE), w_uq.dtype)
    wq_r = jnp.concatenate([zn, rot, pad], axis=-1).reshape(L, MLA_Q_RANK, MLA_HEADS * HEAD_PAD)
    wkv = w_ukv.reshape(L, MLA_KV_RANK, MLA_HEADS, MLA_NOPE + MLA_V)
    zk = jnp.zeros((L, MLA_KV_RANK, MLA_HEADS, HEAD_PAD - MLA_NOPE), w_ukv.dtype)
    wk = jnp.concatenate([wkv[..., :MLA_NOPE], zk], axis=-1).reshape(L, MLA_KV_RANK, MLA_HEADS * HEAD_PAD)
    wv = wkv[..., MLA_NOPE:].reshape(L, MLA_KV_RANK, MLA_HEADS * MLA_V)
    return wq_p.astype(BF16), wq_r.astype(BF16), wk.astype(BF16), wv.astype(BF16)


def _ret_tables(ret_log_decay):
    c = RET_CHUNK
    lg = jnp.log1p(-jnp.exp(ret_log_decay.astype(F32)))
    lg_f, lg_b = lg[:, 0, :, None, None], lg[:, 1, :, None, None]
    idx = jnp.arange(c, dtype=F32)
    diff = idx[:, None] - idx[None, :]
    dmask = jnp.where(diff >= 0, jnp.exp(lg_f * jnp.maximum(diff, 0.0)),
                      jnp.exp(lg_b * jnp.maximum(-diff, 0.0)))
    lf, lb = lg[:, 0, :, None], lg[:, 1, :, None]
    ones = jnp.ones((c,), F32)
    cols = [jnp.exp(lf * (idx + 1)), jnp.exp(lf * (c - 1 - idx)), jnp.exp(lb * (c - idx)),
            jnp.exp(lb * idx), jnp.exp(lf * c) * ones, jnp.exp(lb * c) * ones,
            0.0 * lf * ones, 0.0 * lf * ones]
    return dmask, jnp.stack(cols, axis=-1)


def _s5_tables(lam_re, lam_im, log_step, b_re, b_im, c_re, c_im):
    L = lam_re.shape[0]
    G, P, CH, C = S5_GROUPS, S5_STATE, S5_GROUP_CH, S5_CHUNK
    f32 = lambda t: t.astype(F32)
    lam_re, lam_im, b_re, b_im, c_re, c_im = map(f32, (lam_re, lam_im, b_re, b_im, c_re, c_im))
    step = jnp.exp(f32(log_step))[..., None]
    mag = jnp.exp(lam_re * step)
    a_re, a_im = mag * jnp.cos(lam_im * step), mag * jnp.sin(lam_im * step)
    den = jnp.square(lam_re) + jnp.square(lam_im)
    f_re = ((a_re - 1.0) * lam_re + a_im * lam_im) / den
    f_im = (a_im * lam_re - (a_re - 1.0) * lam_im) / den
    bb_re = f_re[..., None] * b_re - f_im[..., None] * b_im
    bb_im = f_re[..., None] * b_im + f_im[..., None] * b_re
    k = jnp.arange(C + 1, dtype=F32)[:, None, None, None, None]
    pm = jnp.exp(k * (lam_re * step))
    pw_re, pw_im = pm * jnp.cos(k * (lam_im * step)), pm * jnp.sin(k * (lam_im * step))

    e_re = pw_re[..., None] * bb_re - pw_im[..., None] * bb_im
    e_im = pw_re[..., None] * bb_im + pw_im[..., None] * bb_re
    cr = jnp.swapaxes(c_re, -1, -2)
    ci = jnp.swapaxes(c_im, -1, -2)
    h_re = pw_re[..., None] * cr - pw_im[..., None] * ci
    h_im = pw_re[..., None] * ci + pw_im[..., None] * cr

    kern = (jnp.einsum('lzgpo,dlzgpi->dlzgoi', cr, e_re[:C], precision=HI)
            - jnp.einsum('lzgpo,dlzgpi->dlzgoi', ci, e_im[:C], precision=HI))
    jj, ii = np.meshgrid(np.arange(C), np.arange(C), indexing='ij')
    t_f = jnp.where((ii >= jj)[:, :, None, None, None, None], kern[np.clip(ii - jj, 0, C - 1), :, 0], 0.0)
    t_b = jnp.where((jj >= ii)[:, :, None, None, None, None], kern[np.clip(jj - ii, 0, C - 1), :, 1], 0.0)
    t_g = (t_f + t_b).transpose(2, 3, 0, 5, 1, 4)
    t_g = t_g.reshape(L, G, C * CH, C * CH)

    jdx = np.arange(C)
    m_in = jnp.stack([e_re[C - 1 - jdx, :, 0], e_im[C - 1 - jdx, :, 0],
                      e_re[jdx, :, 1], e_im[jdx, :, 1]], axis=0)
    m_in = m_in.transpose(2, 3, 1, 5, 0, 4).reshape(L, G, C * CH, 4, P)
    idx = np.arange(C)
    m_out = jnp.stack([h_re[idx + 1, :, 0], -h_im[idx + 1, :, 0],
                       h_re[C - idx, :, 1], -h_im[C - idx, :, 1]], axis=0)
    m_out = m_out.transpose(2, 3, 0, 4, 1, 5).reshape(L, G, 4, P, C * CH)

    eye = jnp.eye(2, dtype=F32)
    NP = S5_PAIRS
    t_p = jnp.einsum('lpgrc,gh->lpgrhc', t_g.reshape(L, NP, 2, C * CH, C * CH), eye)
    t_p = t_p.reshape(L * NP, 2 * C * CH, 2 * C * CH)
    mi_p = jnp.einsum('lpgrqs,gh->lpgrqhs', m_in.reshape(L, NP, 2, C * CH, 4, P), eye)
    mi_p = mi_p.reshape(L * NP, 2 * C * CH, 4 * 2 * P)
    mo_p = jnp.einsum('lpgqsc,gh->lpqgshc', m_out.reshape(L, NP, 2, 4, P, C * CH), eye)
    mo_p = mo_p.reshape(L * NP, 4 * 2 * P, 2 * C * CH)
    a_vec = jnp.stack([pw_re[C, :, 0], pw_im[C, :, 0], pw_re[C, :, 1], pw_im[C, :, 1]], axis=1)
    a_vec = a_vec.reshape(L, 4, G * P)
    return t_p.astype(BF16), mi_p.astype(BF16), mo_p.astype(BF16), a_vec


def _pack_ffn(w_up, conv_w, conv_b, w_down):
    L, D, two_f = w_up.shape
    dff = two_f // 2
    nc = dff // FFN_CHUNK
    chunked = lambda w: w.reshape(L, -1, nc, FFN_CHUNK).swapaxes(1, 2)
    wa, wg = chunked(w_up[..., :dff]).astype(BF16), chunked(w_up[..., dff:]).astype(BF16)
    cw = jnp.concatenate([conv_w, conv_b[:, None, :]], axis=1)
    cwa, cwg = chunked(cw[..., :dff]), chunked(cw[..., dff:])
    wd = w_down.reshape(L, nc, FFN_CHUNK, D).astype(BF16)
    return wa, wg, cwa, cwg, wd


def _tile(n, pref):
    t = min(pref, n)
    assert n % t == 0, (n, t)
    return t


def kernel(x, c, positions, w_in, mla_q_norm, mla_w_uq, mla_kv_norm, mla_w_ukv, ret_log_decay, s5_lam_re, s5_lam_im, s5_log_step, s5_b_re, s5_b_im, s5_c_re, s5_c_im, s5_d, s5_w_glu, w_branch_mla, w_branch_ret, w_branch_s5, w_o, ffn_w_up, ffn_conv_w, ffn_conv_b, ffn_w_down, ln1_g, ln1_b, ln2_g, ln2_b, w_ada, b_ada):
    B, S, D = x.shape
    L = w_in.shape[0]
    T = B * S
    assert S % RET_CHUNK == 0 and S % (16 * S5_CHUNK) == 0 and ffn_w_up.shape[2] % (2 * FFN_CHUNK) == 0
    alpha = (2 * L) ** 0.25
    tm = _tile(S, 512)
    tps = S // tm

    cos_m, sin_m, cos_r, sin_r = _rope_tables(positions, _tile(T, 2048))
    mod = _ada_mod(c, w_ada, b_ada).reshape(L, B, 1, -1)

    w_packed = _pack_w_in(w_in)
    wq, wqr, wk, wv = _pack_mla(mla_w_uq, mla_w_ukv)
    dmask, ret_vec = _ret_tables(ret_log_decay)
    t_mat, m_in, m_out, a_vec = _s5_tables(s5_lam_re, s5_lam_im, s5_log_step, s5_b_re, s5_b_im,
                                           s5_c_re, s5_c_im)
    wa, wg, cwa, cwg, wd = _pack_ffn(ffn_w_up, ffn_conv_w, ffn_conv_b, ffn_w_down)
    bf = lambda w: w.astype(BF16)
    row2 = lambda v: v.reshape(L, 1, -1)
    q_gain, kv_gain = row2(mla_q_norm), row2(mla_kv_norm)
    d_skip = row2(s5_d)
    g1, b1, g2, b2 = row2(ln1_g), row2(ln1_b), row2(ln2_g), row2(ln2_b)
    w_glu, wbm, wbr, wbs, wo = map(bf, (s5_w_glu, w_branch_mla, w_branch_ret, w_branch_s5, w_o))
    NP = S5_PAIRS

    xf = x.reshape(T, D)
    for l in range(L):
        qc, kvc, kr, rq, rk, rv, rg, u, gates = _in_proj(xf, mod[l], w_packed[l], tm, tps)
        q, k, v = _mla_prep(qc, kvc, kr, cos_m, sin_m, q_gain[l], kv_gain[l],
                            wq[l], wqr[l], wk[l], wv[l], tm)
        o_mla = _mla_attn(q, k, v, B, S, _tile(S, 512), _tile(S, 512))
        o_ret = _retention(rq, rk, rv, rg, cos_r, sin_r, dmask[l], ret_vec[l], B, S, tm)
        y_ssm = _s5_core(_to_pairs(u, B, S), m_in[l * NP:(l + 1) * NP], t_mat[l * NP:(l + 1) * NP],
                         m_out[l * NP:(l + 1) * NP], a_vec[l])
        x1 = _merge(xf, mod[l], o_mla, o_ret, _from_pairs(y_ssm, B, S), u, gates, d_skip[l],
                    w_glu[l], wbm[l], wbr[l], wbs[l], wo[l], g1[l], b1[l], tm, tps, alpha)
        xf = _ffn(x1, mod[l], wa[l], wg[l], cwa[l], cwg[l], wd[l], g2[l], b2[l], tm, tps, alpha)
    return xf.reshape(B, S, D)
```

```python
import functools
import math

import jax
import jax.numpy as jnp
import numpy as np
from jax import lax
from jax.experimental import pallas as pl
from jax.experimental.pallas import tpu as pltpu

F32 = jnp.float32
BF16 = jnp.bfloat16
HI = lax.Precision.HIGHEST

MLA_HEADS = 8
MLA_Q_RANK = 256
MLA_KV_RANK = 128
MLA_NOPE = 64
MLA_ROPE = 32
MLA_V = 64
RET_HEADS = 4
RET_DK = 64
RET_DV = 128
RET_CHUNK = 128
S5_GROUP_CH = 16
S5_WIDTH = 512
S5_GROUPS = S5_WIDTH // S5_GROUP_CH
S5_STATE = 64
S5_CHUNK = 16
S5_PAIRS = S5_GROUPS // 2
ROPE_BASE = 10000.0
LN_EPS = 1e-5
RMS_EPS = 1e-6
GN_EPS = 1e-5
FFN_CHUNK = 256

LANES = 128
HEAD_PAD = 128
VMEM_LIMIT = 56 * 1024 * 1024


def _cparams(n_axes):
    return pltpu.CompilerParams(dimension_semantics=("arbitrary",) * n_axes,
                                vmem_limit_bytes=VMEM_LIMIT)


def _const_spec(shape):
    nd = len(shape)
    return pl.BlockSpec(shape, lambda *_: (0,) * nd)


def _layer_norm(x):
    mu = jnp.mean(x, axis=-1, keepdims=True)
    xc = x - mu
    var = jnp.mean(xc * xc, axis=-1, keepdims=True)
    return xc * lax.rsqrt(var + LN_EPS)


def _rms_norm(x, g):
    return x * lax.rsqrt(jnp.mean(x * x, axis=-1, keepdims=True) + RMS_EPS) * g


def _dot(a, b):
    return jnp.dot(a, b, preferred_element_type=F32)


def _dot_nt(a, b):
    return lax.dot_general(a, b, (((1,), (1,)), ((), ())), preferred_element_type=F32)


def _dot_tn(a, b):
    return lax.dot_general(a, b, (((0,), (0,)), ((), ())), preferred_element_type=F32)


def _rope_body(pos_ref, invf_ref, cm_ref, sm_ref, cr_ref, sr_ref):
    p = pos_ref[...].astype(F32)
    ang_m = p * invf_ref[0:1, :]
    ang_r = p * invf_ref[1:2, :]
    cm_ref[...] = jnp.cos(ang_m)
    sm_ref[...] = jnp.sin(ang_m)
    cr_ref[...] = jnp.cos(ang_r)
    sr_ref[...] = jnp.sin(ang_r)


def _rope_tables(positions, tm):
    T = positions.size
    half_m, half_r = MLA_ROPE // 2, RET_DK // 2
    inv_m = ROPE_BASE ** (-jnp.arange(half_m, dtype=F32) / half_m)
    inv_r = ROPE_BASE ** (-jnp.arange(half_r, dtype=F32) / half_r)
    lane = np.arange(LANES)
    row_m = jnp.where((lane >= MLA_NOPE) & (lane < MLA_NOPE + MLA_ROPE),
                      inv_m[(lane - MLA_NOPE) % half_m], 0.0)
    row_r = inv_r[lane % half_r]
    invf = jnp.stack([row_m, row_r]).astype(F32)
    out = jax.ShapeDtypeStruct((T, LANES), F32)
    spec = pl.BlockSpec((tm, LANES), lambda i: (i, 0))
    return pl.pallas_call(
        _rope_body, grid=(T // tm,),
        in_specs=[pl.BlockSpec((tm, 1), lambda i: (i, 0)), _const_spec((2, LANES))],
        out_specs=[spec] * 4, out_shape=[out] * 4,
        compiler_params=_cparams(1), name="rope_tables",
    )(positions.reshape(T, 1), invf)


def _mod_body(c_ref, w_ref, b_ref, o_ref):
    cond = jax.nn.silu(c_ref[...]).astype(BF16)
    o_ref[0] = _dot(cond, w_ref[0].astype(BF16)) + b_ref[0]


def _ada_mod(c, w_ada, b_ada):
    L, D, N = w_ada.shape
    B = c.shape[0]
    tn = D
    return pl.pallas_call(
        _mod_body, grid=(L, N // tn),
        in_specs=[_const_spec((B, D)),
                  pl.BlockSpec((1, D, tn), lambda l, j: (l, 0, j)),
                  pl.BlockSpec((1, 1, tn), lambda l, j: (l, 0, j))],
        out_specs=pl.BlockSpec((1, B, tn), lambda l, j: (l, 0, j)),
        out_shape=jax.ShapeDtypeStruct((L, B, N), F32),
        compiler_params=_cparams(2), name="ada_mod",
    )(c, w_ada, b_ada.reshape(L, 1, N))


_IN_BLOCKS = (("qc", 256, BF16), ("kvc", 128, BF16), ("kr", 128, BF16), ("rq", 256, BF16),
              ("rk", 256, BF16), ("rv", 512, BF16), ("rg", 512, BF16), ("u", 512, F32),
              ("gates", 3072, BF16))


def _in_proj_body(x_ref, mod_ref, w_ref, *out_refs):
    D = x_ref.shape[1]
    xn = _layer_norm(x_ref[...])
    shift = mod_ref[0, :, 0:D]
    scale = mod_ref[0, :, D:2 * D]
    h = (xn * (1.0 + scale) + shift).astype(BF16)
    lo = 0
    for (_, width, _), o_ref in zip(_IN_BLOCKS, out_refs):
        for c0 in range(0, width, 512):
            c1 = min(c0 + 512, width)
            o_ref[:, c0:c1] = _dot(h, w_ref[:, lo + c0:lo + c1]).astype(o_ref.dtype)
        lo += width


def _in_proj(x, mod_l, w_packed, tm, tiles_per_seq):
    T, D = x.shape
    row = lambda i: (i, 0)
    return pl.pallas_call(
        _in_proj_body, grid=(T // tm,),
        in_specs=[pl.BlockSpec((tm, D), row),
                  pl.BlockSpec((1, 1, mod_l.shape[2]), lambda i: (i // tiles_per_seq, 0, 0)),
                  _const_spec(w_packed.shape)],
        out_specs=[pl.BlockSpec((tm, w), row) for _, w, _ in _IN_BLOCKS],
        out_shape=[jax.ShapeDtypeStruct((T, w), dt) for _, w, dt in _IN_BLOCKS],
        compiler_params=_cparams(1), name="in_proj",
    )(x, mod_l, w_packed)


def _rot_half(x, lo, half):
    lane = lax.broadcasted_iota(jnp.int32, x.shape, 1)
    up = pltpu.roll(x, LANES - half, 1)
    down = pltpu.roll(x, half, 1)
    if lo is None:
        return jnp.where(lane % (2 * half) < half, -up, down)
    return jnp.where((lane >= lo) & (lane < lo + half), -up,
                     jnp.where((lane >= lo + half) & (lane < lo + 2 * half), down, 0.0))


def _mla_rope(x, cos, sin):
    tiles = [x[:, i * HEAD_PAD:(i + 1) * HEAD_PAD] for i in range(x.shape[1] // HEAD_PAD)]
    return jnp.concatenate([t * cos + _rot_half(t, MLA_NOPE, MLA_ROPE // 2) * sin for t in tiles],
                           axis=1)


def _mla_prep_body(qc_ref, kvc_ref, kr_ref, cos_ref, sin_ref, qg_ref, kvg_ref,
                   wq_ref, wk_ref, wv_ref, q_ref, k_ref, v_ref):
    cos, sin = cos_ref[...], sin_ref[...]
    qn = _rms_norm(qc_ref[...].astype(F32), qg_ref[...]).astype(BF16)
    scale = (MLA_NOPE + MLA_ROPE) ** -0.5 * math.log2(math.e)
    q_ref[...] = (_mla_rope(_dot(qn, wq_ref[...]), cos, sin) * scale).astype(BF16)
    kvn = _rms_norm(kvc_ref[...].astype(F32), kvg_ref[...]).astype(BF16)
    k_rope = _mla_rope(kr_ref[...].astype(F32), cos, sin)
    k = _dot(kvn, wk_ref[...]) + jnp.concatenate([k_rope] * MLA_HEADS, axis=1)
    k_ref[...] = k.astype(BF16)
    v_ref[...] = _dot(kvn, wv_ref[...]).astype(BF16)


def _mla_prep(qc, kvc, kr, cos_m, sin_m, q_gain, kv_gain, wq, wk, wv, tm):
    T = qc.shape[0]
    row = lambda i: (i, 0)
    qk_w = MLA_HEADS * HEAD_PAD
    v_w = MLA_HEADS * MLA_V
    return pl.pallas_call(
        _mla_prep_body, grid=(T // tm,),
        in_specs=[pl.BlockSpec((tm, qc.shape[1]), row), pl.BlockSpec((tm, kvc.shape[1]), row),
                  pl.BlockSpec((tm, kr.shape[1]), row),
                  pl.BlockSpec((tm, LANES), row), pl.BlockSpec((tm, LANES), row),
                  _const_spec(q_gain.shape), _const_spec(kv_gain.shape),
                  _const_spec(wq.shape), _const_spec(wk.shape), _const_spec(wv.shape)],
        out_specs=[pl.BlockSpec((tm, qk_w), row), pl.BlockSpec((tm, qk_w), row),
                   pl.BlockSpec((tm, v_w), row)],
        out_shape=[jax.ShapeDtypeStruct((T, qk_w), BF16), jax.ShapeDtypeStruct((T, qk_w), BF16),
                   jax.ShapeDtypeStruct((T, v_w), BF16)],
        compiler_params=_cparams(1), name="mla_prep",
    )(qc, kvc, kr, cos_m, sin_m, q_gain, kv_gain, wq, wk, wv)


def _attn_body(q_ref, k_ref, v_ref, o_ref, m_ref, l_ref, acc_ref, *, tk):
    tq = q_ref.shape[0]
    nk = k_ref.shape[0] // tk
    nt = tk // LANES
    heads = [slice(j * HEAD_PAD, (j + 1) * HEAD_PAD) for j in range(2)]
    m_ref[...] = jnp.full(m_ref.shape, -jnp.inf, F32)
    l_ref[...] = jnp.zeros(l_ref.shape, F32)
    acc_ref[...] = jnp.zeros(acc_ref.shape, F32)

    def kstep(t, carry):
        r0 = pl.multiple_of(t * tk, tk)
        v = v_ref[pl.ds(r0, tk), :]
        s = [_dot_nt(q_ref[:, heads[j]], k_ref[pl.ds(r0, tk), heads[j]]) for j in range(2)]
        for j in range(2):
            tiles = [s[j][:, i * LANES:(i + 1) * LANES] for i in range(nt)]
            m_prev = m_ref[j]
            m_new = jnp.maximum(m_prev, jnp.max(functools.reduce(jnp.maximum, tiles),
                                                axis=1, keepdims=True))
            alpha = jnp.exp2(m_prev - m_new)
            p = [jnp.exp2(t_ - m_new) for t_ in tiles]
            l_ref[j] = alpha * l_ref[j] + functools.reduce(jnp.add, p)
            m_ref[j] = m_new
            pb = jnp.concatenate([t_.astype(BF16) for t_ in p], axis=1)
            acc_ref[j] = alpha * acc_ref[j] + _dot(pb, v)
        return carry

    lax.fori_loop(0, nk, kstep, 0, unroll=4)
    outs = [acc_ref[j] / jnp.sum(l_ref[j], axis=1, keepdims=True) for j in range(2)]
    lane = lax.broadcasted_iota(jnp.int32, (tq, 2 * MLA_V), 1)
    o_ref[...] = jnp.where(lane < MLA_V, outs[0], outs[1]).astype(o_ref.dtype)


def _mla_attn(q, k, v, B, S, tq, tk):
    T = q.shape[0]
    nq = S // tq
    pairs = MLA_HEADS // 2
    return pl.pallas_call(
        functools.partial(_attn_body, tk=tk), grid=(B, pairs, nq),
        in_specs=[pl.BlockSpec((tq, 2 * HEAD_PAD), lambda b, h, i: (b * nq + i, h)),
                  pl.BlockSpec((S, 2 * HEAD_PAD), lambda b, h, i: (b, h)),
                  pl.BlockSpec((S, 2 * MLA_V), lambda b, h, i: (b, h))],
        out_specs=pl.BlockSpec((tq, 2 * MLA_V), lambda b, h, i: (b * nq + i, h)),
        out_shape=jax.ShapeDtypeStruct((T, MLA_HEADS * MLA_V), BF16),
        scratch_shapes=[pltpu.VMEM((2, tq, LANES), F32), pltpu.VMEM((2, tq, LANES), F32),
                        pltpu.VMEM((2, tq, 2 * MLA_V), F32)],
        compiler_params=_cparams(3), name="mla_attn",
    )(q, k, v)


_XI_F, _ZETA_F, _XI_B, _ZETA_B, _CARRY_F, _CARRY_B = range(6)


def _ret_rope(r_ref, cos, sin):
    r = r_ref[...].astype(F32)
    tiles = [r[:, i * LANES:(i + 1) * LANES] for i in range(r.shape[1] // LANES)]
    return jnp.concatenate([t * cos + _rot_half(t, None, RET_DK // 2) * sin for t in tiles], axis=1)


def _head_masked(x_pair, h):
    lane = lax.broadcasted_iota(jnp.int32, x_pair.shape, 1)
    lo = (h % 2) * RET_DK
    return jnp.where((lane >= lo) & (lane < lo + RET_DK), x_pair, 0.0)


def _ret_fwd_body(rq_ref, rk_ref, rv_ref, cos_ref, sin_ref, dm_ref, vec_ref, y_ref, s_ref):
    @pl.when(pl.program_id(1) == 0)
    def _():
        s_ref[...] = jnp.zeros(s_ref.shape, F32)

    c = RET_CHUNK
    q = _ret_rope(rq_ref, cos_ref[...], sin_ref[...])
    k = _ret_rope(rk_ref, cos_ref[...], sin_ref[...])
    for ci in range(rq_ref.shape[0] // c):
        rows = slice(ci * c, (ci + 1) * c)
        for h in range(RET_HEADS):
            pair = slice((h // 2) * LANES, (h // 2 + 1) * LANES)
            vcols = slice(h * RET_DV, (h + 1) * RET_DV)
            vec = vec_ref[h]
            qm = _head_masked(q[rows, pair], h)
            kp = k[rows, pair]
            vh = rv_ref[rows, vcols]
            scores = _dot_nt(qm.astype(BF16), kp.astype(BF16)) * dm_ref[h]
            y = _dot(scores.astype(BF16), vh)
            state = s_ref[h]
            y = y + _dot((qm * vec[:, _XI_F:_XI_F + 1]).astype(BF16), state.astype(BF16))
            y_ref[rows, vcols] = y
            kz = (kp * vec[:, _ZETA_F:_ZETA_F + 1]).astype(BF16)
            s_ref[h] = vec[0:1, _CARRY_F:_CARRY_F + 1] * state + _dot_tn(kz, vh)


def _ret_bwd_body(rq_ref, rk_ref, rv_ref, rg_ref, ya_ref, cos_ref, sin_ref, vec_ref, o_ref, s_ref):
    @pl.when(pl.program_id(1) == 0)
    def _():
        s_ref[...] = jnp.zeros(s_ref.shape, F32)

    c = RET_CHUNK
    q = _ret_rope(rq_ref, cos_ref[...], sin_ref[...])
    k = _ret_rope(rk_ref, cos_ref[...], sin_ref[...])
    for ci in reversed(range(rq_ref.shape[0] // c)):
        rows = slice(ci * c, (ci + 1) * c)
        for h in range(RET_HEADS):
            pair = slice((h // 2) * LANES, (h // 2 + 1) * LANES)
            vcols = slice(h * RET_DV, (h + 1) * RET_DV)
            vec = vec_ref[h]
            qm = _head_masked(q[rows, pair], h)
            kp = k[rows, pair]
            vh = rv_ref[rows, vcols]
            state = s_ref[h]
            y = ya_ref[rows, vcols] + _dot((qm * vec[:, _XI_B:_XI_B + 1]).astype(BF16),
                                           state.astype(BF16))
            kz = (kp * vec[:, _ZETA_B:_ZETA_B + 1]).astype(BF16)
            s_ref[h] = vec[0:1, _CARRY_B:_CARRY_B + 1] * state + _dot_tn(kz, vh)
            mu = jnp.mean(y, axis=-1, keepdims=True)
            yc = y - mu
            var = jnp.mean(yc * yc, axis=-1, keepdims=True)
            yn = yc * lax.rsqrt(var + GN_EPS)
            gate = jax.nn.silu(rg_ref[rows, vcols].astype(F32))
            o_ref[rows, vcols] = (gate * yn).astype(o_ref.dtype)


def _retention(rq, rk, rv, rg, cos_r, sin_r, dmask, vec, B, S, tr):
    T = rq.shape[0]
    nb = S // tr
    w = RET_HEADS * RET_DV
    fwd = lambda b, j: (b * nb + j, 0)
    bwd = lambda b, j: (b * nb + nb - 1 - j, 0)
    state = pltpu.VMEM((RET_HEADS, LANES, RET_DV), F32)
    ya = pl.pallas_call(
        _ret_fwd_body, grid=(B, nb),
        in_specs=[pl.BlockSpec((tr, rq.shape[1]), fwd), pl.BlockSpec((tr, rk.shape[1]), fwd),
                  pl.BlockSpec((tr, w), fwd), pl.BlockSpec((tr, LANES), fwd),
                  pl.BlockSpec((tr, LANES), fwd), _const_spec(dmask.shape), _const_spec(vec.shape)],
        out_specs=pl.BlockSpec((tr, w), fwd),
        out_shape=jax.ShapeDtypeStruct((T, w), F32),
        scratch_shapes=[state], compiler_params=_cparams(2), name="ret_fwd",
    )(rq, rk, rv, cos_r, sin_r, dmask, vec)
    return pl.pallas_call(
        _ret_bwd_body, grid=(B, nb),
        in_specs=[pl.BlockSpec((tr, rq.shape[1]), bwd), pl.BlockSpec((tr, rk.shape[1]), bwd),
                  pl.BlockSpec((tr, w), bwd), pl.BlockSpec((tr, w), bwd), pl.BlockSpec((tr, w), bwd),
                  pl.BlockSpec((tr, LANES), bwd), pl.BlockSpec((tr, LANES), bwd),
                  _const_spec(vec.shape)],
        out_specs=pl.BlockSpec((tr, w), bwd),
        out_shape=jax.ShapeDtypeStruct((T, w), BF16),
        scratch_shapes=[state], compiler_params=_cparams(2), name="ret_bwd",
    )(rq, rk, rv, rg, ya, cos_r, sin_r, vec)


PAIR_W = 2 * S5_GROUP_CH
PAIRS_PER_TILE = LANES // PAIR_W
CHUNK_W = S5_CHUNK * PAIR_W


def _chunk_rows(u_ref, n):
    tok = [u_ref[pl.ds(i, n, stride=S5_CHUNK), :] for i in range(S5_CHUNK)]
    return [jnp.concatenate([t[:, p * PAIR_W:(p + 1) * PAIR_W] for t in tok], axis=1)
            for p in range(PAIRS_PER_TILE)]


def _s5_in_body(u_ref, min_ref, fr_ref, fi_ref, br_ref, bi_ref):
    n = fr_ref.shape[1]
    for p, rows in enumerate(_chunk_rows(u_ref, n)):
        v = _dot(rows.astype(BF16), min_ref[p])
        for part, ref in enumerate((fr_ref, fi_ref, br_ref, bi_ref)):
            ref[0, :, p * LANES:(p + 1) * LANES] = v[:, part * LANES:(part + 1) * LANES]


def _s5_scan_body(fr_ref, fi_ref, br_ref, bi_ref, a_ref, zfr_ref, zfi_ref, zbr_ref, zbi_ref, *, rb):
    n = fr_ref.shape[1]
    nblk = n // rb
    lw = fr_ref.shape[2]
    afr, afi = a_ref[0:1, :], a_ref[1:2, :]
    abr, abi = a_ref[2:3, :], a_ref[3:4, :]
    zero = jnp.zeros((1, lw), F32)

    def step(blk, carry):
        xr, xi, yr, yi = carry
        r0 = pl.multiple_of(blk * rb, rb)
        vr, vi = fr_ref[0, pl.ds(r0, rb), :], fi_ref[0, pl.ds(r0, rb), :]
        out_r, out_i = [], []
        for r in range(rb):
            out_r.append(xr)
            out_i.append(xi)
            xr, xi = (afr * xr - afi * xi + vr[r:r + 1], afr * xi + afi * xr + vi[r:r + 1])
        zfr_ref[0, pl.ds(r0, rb), :] = jnp.concatenate(out_r, axis=0).astype(zfr_ref.dtype)
        zfi_ref[0, pl.ds(r0, rb), :] = jnp.concatenate(out_i, axis=0).astype(zfi_ref.dtype)
        r1 = pl.multiple_of((nblk - 1 - blk) * rb, rb)
        wr, wi = br_ref[0, pl.ds(r1, rb), :], bi_ref[0, pl.ds(r1, rb), :]
        out_r, out_i = [None] * rb, [None] * rb
        for r in reversed(range(rb)):
            out_r[r] = yr
            out_i[r] = yi
            yr, yi = (abr * yr - abi * yi + wr[r:r + 1], abr * yi + abi * yr + wi[r:r + 1])
        zbr_ref[0, pl.ds(r1, rb), :] = jnp.concatenate(out_r, axis=0).astype(zbr_ref.dtype)
        zbi_ref[0, pl.ds(r1, rb), :] = jnp.concatenate(out_i, axis=0).astype(zbi_ref.dtype)
        return xr, xi, yr, yi

    lax.fori_loop(0, nblk, step, (zero, zero, zero, zero))


def _s5_out_body(u_ref, zfr_ref, zfi_ref, zbr_ref, zbi_ref, t_ref, mout_ref, y_ref):
    n = zfr_ref.shape[1]
    ys = []
    for p, rows in enumerate(_chunk_rows(u_ref, n)):
        cols = slice(p * LANES, (p + 1) * LANES)
        z = jnp.concatenate([r[0, :, cols] for r in (zfr_ref, zfi_ref, zbr_ref, zbi_ref)], axis=1)
        ys.append(_dot(rows.astype(BF16), t_ref[p]) + _dot(z, mout_ref[p]))
    for i in range(S5_CHUNK):
        y_ref[pl.ds(i, n, stride=S5_CHUNK), :] = jnp.concatenate(
            [y[:, i * PAIR_W:(i + 1) * PAIR_W] for y in ys], axis=1)


def _s5_core(u, m_in, t_mat, m_out, a_vec, B, S):
    N = S // S5_CHUNK
    n_tiles = S5_WIDTH // LANES
    sw = S5_GROUPS * S5_STATE
    pw = PAIRS_PER_TILE * LANES
    u_spec = pl.BlockSpec((S, LANES), lambda b, q: (b, q))
    part_spec = pl.BlockSpec((1, N, pw), lambda b, q: (b, 0, q))
    mat_spec = pl.BlockSpec((PAIRS_PER_TILE, CHUNK_W, CHUNK_W), lambda b, q: (q, 0, 0))
    parts = pl.pallas_call(
        _s5_in_body, grid=(B, n_tiles),
        in_specs=[u_spec, mat_spec], out_specs=[part_spec] * 4,
        out_shape=[jax.ShapeDtypeStruct((B, N, sw), F32)] * 4,
        compiler_params=_cparams(2), name="s5_in",
    )(u, m_in)
    lw = 512
    blk = pl.BlockSpec((1, N, lw), lambda b, j: (b, 0, j))
    z = pl.pallas_call(
        functools.partial(_s5_scan_body, rb=16), grid=(B, sw // lw),
        in_specs=[blk] * 4 + [pl.BlockSpec((4, lw), lambda b, j: (0, j))],
        out_specs=[blk] * 4, out_shape=[jax.ShapeDtypeStruct((B, N, sw), BF16)] * 4,
        compiler_params=_cparams(2), name="s5_scan",
    )(*parts, a_vec)
    return pl.pallas_call(
        _s5_out_body, grid=(B, n_tiles),
        in_specs=[u_spec] + [part_spec] * 4 + [mat_spec, mat_spec],
        out_specs=u_spec, out_shape=jax.ShapeDtypeStruct((B * S, S5_WIDTH), F32),
        compiler_params=_cparams(2), name="s5_out",
    )(u, *z, t_mat, m_out)


def _merge_body(x_ref, mod_ref, om_ref, or_ref, ys_ref, u_ref, g_ref, d_ref, wglu_ref,
                wm_ref, wr_ref, ws_ref, wo_ref, lng_ref, lnb_ref, o_ref, *, alpha):
    D = x_ref.shape[1]
    y = d_ref[...] * u_ref[...].astype(F32) + ys_ref[...].astype(F32)
    y = jax.nn.gelu(y)
    o_s5 = y * jax.nn.sigmoid(_dot(y.astype(BF16), wglu_ref[...]))
    merged = (jax.nn.sigmoid(g_ref[:, 0:D].astype(F32)) * _dot(om_ref[...], wm_ref[...])
              + jax.nn.sigmoid(g_ref[:, D:2 * D].astype(F32)) * _dot(or_ref[...], wr_ref[...])
              + jax.nn.sigmoid(g_ref[:, 2 * D:3 * D].astype(F32)) * _dot(o_s5.astype(BF16), ws_ref[...]))
    out = _dot(merged.astype(BF16), wo_ref[...])
    gate1 = mod_ref[0, :, 2 * D:3 * D]
    o_ref[...] = _layer_norm(alpha * x_ref[...] + gate1 * out) * lng_ref[...] + lnb_ref[...]


def _merge(x, mod_l, o_mla, o_ret, y_ssm, u, gates, d_skip, w_glu, wm, wr, ws, wo, ln_g, ln_b,
           tm, tiles_per_seq, alpha):
    T, D = x.shape
    row = lambda i: (i, 0)
    consts = (d_skip, w_glu, wm, wr, ws, wo, ln_g, ln_b)
    return pl.pallas_call(
        functools.partial(_merge_body, alpha=alpha), grid=(T // tm,),
        in_specs=[pl.BlockSpec((tm, D), row),
                  pl.BlockSpec((1, 1, mod_l.shape[2]), lambda i: (i // tiles_per_seq, 0, 0))]
                 + [pl.BlockSpec((tm, a.shape[1]), row) for a in (o_mla, o_ret, y_ssm, u, gates)]
                 + [_const_spec(a.shape) for a in consts],
        out_specs=pl.BlockSpec((tm, D), row),
        out_shape=jax.ShapeDtypeStruct((T, D), F32),
        compiler_params=_cparams(1), name="merge",
    )(x, mod_l, o_mla, o_ret, y_ssm, u, gates, *consts)


HALO = 8


def _ffn_body(x_ref, xp_ref, xn_ref, mod_ref, wup_ref, cw_ref, wd_ref,
              lng_ref, lnb_ref, o_ref, *, alpha, tiles_per_seq):
    tm, D = x_ref.shape
    dff = wd_ref.shape[0]
    i = pl.program_id(0)
    shift = mod_ref[0, :, 3 * D:4 * D]
    scale = mod_ref[0, :, 4 * D:5 * D]
    gate2 = mod_ref[0, :, 5 * D:6 * D]
    x = x_ref[...]
    x_ext = jnp.concatenate([xp_ref[...], x, xn_ref[...]], axis=0)
    h = _layer_norm(x_ext) * (1.0 + scale) + shift
    r = lax.broadcasted_iota(jnp.int32, (tm + 2 * HALO, 1), 0)
    pos_in_seq = i % tiles_per_seq
    lo = jnp.where(pos_in_seq == 0, HALO, 0)
    hi = jnp.where(pos_in_seq == tiles_per_seq - 1, tm + HALO, tm + 2 * HALO)
    h = jnp.where((r >= lo) & (r < hi), h, 0.0).astype(BF16)
    n_ext = tm + 2 * HALO

    def conv_up(c0):
        cols = slice(c0, c0 + FFN_CHUNK)
        up = _dot(h, wup_ref[:, cols])
        prev = pltpu.roll(up, 1, 0)
        nxt = pltpu.roll(up, n_ext - 1, 0)
        y = (prev * cw_ref[0:1, cols] + up * cw_ref[1:2, cols] + nxt * cw_ref[2:3, cols]
             + cw_ref[3:4, cols])
        return y[HALO:HALO + tm, :]

    f = None
    for c0 in range(0, dff, FFN_CHUNK):
        act = (jax.nn.silu(conv_up(dff + c0)) * conv_up(c0)).astype(BF16)
        part = _dot(act, wd_ref[c0:c0 + FFN_CHUNK, :])
        f = part if f is None else f + part
    o_ref[...] = _layer_norm(alpha * x + gate2 * f) * lng_ref[...] + lnb_ref[...]


def _ffn(x, mod_l, w_up, cw, wd, ln_g, ln_b, tm, tiles_per_seq, alpha):
    T, D = x.shape
    hb = tm // HALO
    n_hb = T // HALO
    consts = (w_up, cw, wd, ln_g, ln_b)
    return pl.pallas_call(
        functools.partial(_ffn_body, alpha=alpha, tiles_per_seq=tiles_per_seq), grid=(T // tm,),
        in_specs=[pl.BlockSpec((tm, D), lambda i: (i, 0)),
                  pl.BlockSpec((HALO, D), lambda i: (jnp.maximum(i * hb - 1, 0), 0)),
                  pl.BlockSpec((HALO, D), lambda i: (jnp.minimum((i + 1) * hb, n_hb - 1), 0)),
                  pl.BlockSpec((1, 1, mod_l.shape[2]), lambda i: (i // tiles_per_seq, 0, 0))]
                 + [_const_spec(a.shape) for a in consts],
        out_specs=pl.BlockSpec((tm, D), lambda i: (i, 0)),
        out_shape=jax.ShapeDtypeStruct((T, D), F32),
        compiler_params=_cparams(1), name="ffn",
    )(x, x, x, mod_l, *consts)


def _pack_w_in(w_in):
    widths = [MLA_Q_RANK, MLA_KV_RANK, MLA_ROPE, RET_HEADS * RET_DK, RET_HEADS * RET_DK,
              RET_HEADS * RET_DV, RET_HEADS * RET_DV, S5_WIDTH]
    sp = [0] + [int(v) for v in np.cumsum(widths)]
    w_qc, w_kvc, w_kr, w_rq, w_rk, w_rv, w_rg, w_u = [w_in[..., sp[i]:sp[i + 1]] for i in range(8)]
    w_gates = w_in[..., sp[8]:]
    w_kr = jnp.pad(w_kr, ((0, 0), (0, 0), (MLA_NOPE, HEAD_PAD - MLA_NOPE - MLA_ROPE)))
    blocks = [w_qc, w_kvc, w_kr, w_rq * (RET_DK ** -0.5), w_rk, w_rv, w_rg, w_u, w_gates]
    return jnp.concatenate(blocks, axis=-1).astype(BF16)


def _pack_mla(w_uq, w_ukv):
    L = w_uq.shape[0]
    dq = MLA_NOPE + MLA_ROPE
    wq = w_uq.reshape(L, MLA_Q_RANK, MLA_HEADS, dq)
    wq = jnp.pad(wq, ((0, 0), (0, 0), (0, 0), (0, HEAD_PAD - dq)))
    wkv = w_ukv.reshape(L, MLA_KV_RANK, MLA_HEADS, MLA_NOPE + MLA_V)
    wk = jnp.pad(wkv[..., :MLA_NOPE], ((0, 0), (0, 0), (0, 0), (0, HEAD_PAD - MLA_NOPE)))
    wv = wkv[..., MLA_NOPE:].reshape(L, MLA_KV_RANK, MLA_HEADS * MLA_V)
    flat = lambda w: w.reshape(L, w.shape[1], MLA_HEADS * HEAD_PAD).astype(BF16)
    return flat(wq), flat(wk), wv.astype(BF16)


def _ret_tables(ret_log_decay):
    c = RET_CHUNK
    lg = jnp.log1p(-jnp.exp(ret_log_decay.astype(F32)))
    lg_f, lg_b = lg[:, 0, :, None, None], lg[:, 1, :, None, None]
    idx = jnp.arange(c, dtype=F32)
    diff = idx[:, None] - idx[None, :]
    dmask = jnp.where(diff >= 0, jnp.exp(lg_f * jnp.maximum(diff, 0.0)),
                      jnp.exp(lg_b * jnp.maximum(-diff, 0.0)))
    lf, lb = lg[:, 0, :, None], lg[:, 1, :, None]
    ones = jnp.ones((c,), F32)
    cols = [jnp.exp(lf * (idx + 1)), jnp.exp(lf * (c - 1 - idx)), jnp.exp(lb * (c - idx)),
            jnp.exp(lb * idx), jnp.exp(lf * c) * ones, jnp.exp(lb * c) * ones,
            0.0 * lf * ones, 0.0 * lf * ones]
    return dmask, jnp.stack(cols, axis=-1)


def _s5_tables(lam_re, lam_im, log_step, b_re, b_im, c_re, c_im):
    L = lam_re.shape[0]
    G, P, CH, C = S5_GROUPS, S5_STATE, S5_GROUP_CH, S5_CHUNK
    f32 = lambda t: t.astype(F32)
    lam_re, lam_im, b_re, b_im, c_re, c_im = map(f32, (lam_re, lam_im, b_re, b_im, c_re, c_im))
    step = jnp.exp(f32(log_step))[..., None]
    mag = jnp.exp(lam_re * step)
    a_re, a_im = mag * jnp.cos(lam_im * step), mag * jnp.sin(lam_im * step)
    den = jnp.square(lam_re) + jnp.square(lam_im)
    f_re = ((a_re - 1.0) * lam_re + a_im * lam_im) / den
    f_im = (a_im * lam_re - (a_re - 1.0) * lam_im) / den
    bb_re = f_re[..., None] * b_re - f_im[..., None] * b_im
    bb_im = f_re[..., None] * b_im + f_im[..., None] * b_re
    k = jnp.arange(C + 1, dtype=F32)[:, None, None, None, None]
    pm = jnp.exp(k * (lam_re * step))
    pw_re, pw_im = pm * jnp.cos(k * (lam_im * step)), pm * jnp.sin(k * (lam_im * step))

    e_re = pw_re[..., None] * bb_re - pw_im[..., None] * bb_im
    e_im = pw_re[..., None] * bb_im + pw_im[..., None] * bb_re
    cr = jnp.swapaxes(c_re, -1, -2)
    ci = jnp.swapaxes(c_im, -1, -2)
    h_re = pw_re[..., None] * cr - pw_im[..., None] * ci
    h_im = pw_re[..., None] * ci + pw_im[..., None] * cr

    kern = (jnp.einsum('lzgpo,dlzgpi->dlzgoi', cr, e_re[:C], precision=HI)
            - jnp.einsum('lzgpo,dlzgpi->dlzgoi', ci, e_im[:C], precision=HI))
    jj, ii = np.meshgrid(np.arange(C), np.arange(C), indexing='ij')
    t_f = jnp.where((ii >= jj)[:, :, None, None, None, None], kern[np.clip(ii - jj, 0, C - 1), :, 0], 0.0)
    t_b = jnp.where((jj >= ii)[:, :, None, None, None, None], kern[np.clip(jj - ii, 0, C - 1), :, 1], 0.0)
    t_g = (t_f + t_b).transpose(2, 3, 0, 5, 1, 4)
    t_g = t_g.reshape(L, G, C * CH, C * CH)

    jdx = np.arange(C)
    m_in = jnp.stack([e_re[C - 1 - jdx, :, 0], e_im[C - 1 - jdx, :, 0],
                      e_re[jdx, :, 1], e_im[jdx, :, 1]], axis=0)
    m_in = m_in.transpose(2, 3, 1, 5, 0, 4).reshape(L, G, C * CH, 4, P)
    idx = np.arange(C)
    m_out = jnp.stack([h_re[idx + 1, :, 0], -h_im[idx + 1, :, 0],
                       h_re[C - idx, :, 1], -h_im[C - idx, :, 1]], axis=0)
    m_out = m_out.transpose(2, 3, 0, 4, 1, 5).reshape(L, G, 4, P, C * CH)

    eye = jnp.eye(2, dtype=F32)
    NP = S5_PAIRS
    t_p = jnp.einsum('lpgjkic,gh->lpjgkihc', t_g.reshape(L, NP, 2, C, CH, C, CH), eye)
    t_p = t_p.reshape(L * NP, 2 * C * CH, 2 * C * CH)
    mi_p = jnp.einsum('lpgjkqs,gh->lpjgkqhs', m_in.reshape(L, NP, 2, C, CH, 4, P), eye)
    mi_p = mi_p.reshape(L * NP, 2 * C * CH, 4 * 2 * P)
    mo_p = jnp.einsum('lpgqsic,gh->lpqgsihc', m_out.reshape(L, NP, 2, 4, P, C, CH), eye)
    mo_p = mo_p.reshape(L * NP, 4 * 2 * P, 2 * C * CH)
    a_vec = jnp.stack([pw_re[C, :, 0], pw_im[C, :, 0], pw_re[C, :, 1], pw_im[C, :, 1]], axis=1)
    a_vec = a_vec.reshape(L, 4, G * P)
    return t_p.astype(BF16), mi_p.astype(BF16), mo_p.astype(BF16), a_vec


def _pack_ffn(w_up, conv_w, conv_b, w_down):
    cw = jnp.concatenate([conv_w, conv_b[:, None, :]], axis=1)
    return w_up.astype(BF16), cw, w_down.astype(BF16)


def _tile(n, pref):
    t = min(pref, n)
    assert n % t == 0, (n, t)
    return t


def kernel(x, c, positions, w_in, mla_q_norm, mla_w_uq, mla_kv_norm, mla_w_ukv, ret_log_decay, s5_lam_re, s5_lam_im, s5_log_step, s5_b_re, s5_b_im, s5_c_re, s5_c_im, s5_d, s5_w_glu, w_branch_mla, w_branch_ret, w_branch_s5, w_o, ffn_w_up, ffn_conv_w, ffn_conv_b, ffn_w_down, ln1_g, ln1_b, ln2_g, ln2_b, w_ada, b_ada):
    B, S, D = x.shape
    L = w_in.shape[0]
    T = B * S
    assert S % RET_CHUNK == 0 and S % (16 * S5_CHUNK) == 0 and ffn_w_up.shape[2] % (2 * FFN_CHUNK) == 0
    alpha = (2 * L) ** 0.25
    tm = _tile(S, 512)
    tps = S // tm

    cos_m, sin_m, cos_r, sin_r = _rope_tables(positions, _tile(T, 2048))
    mod = _ada_mod(c, w_ada, b_ada).reshape(L, B, 1, -1)

    w_packed = _pack_w_in(w_in)
    wq, wk, wv = _pack_mla(mla_w_uq, mla_w_ukv)
    dmask, ret_vec = _ret_tables(ret_log_decay)
    t_mat, m_in, m_out, a_vec = _s5_tables(s5_lam_re, s5_lam_im, s5_log_step, s5_b_re, s5_b_im,
                                           s5_c_re, s5_c_im)
    w_up, conv_cw, w_down = _pack_ffn(ffn_w_up, ffn_conv_w, ffn_conv_b, ffn_w_down)
    bf = lambda w: w.astype(BF16)
    row2 = lambda v: v.reshape(L, 1, -1)
    q_gain, kv_gain = row2(mla_q_norm), row2(mla_kv_norm)
    d_skip = row2(s5_d)
    g1, b1, g2, b2 = row2(ln1_g), row2(ln1_b), row2(ln2_g), row2(ln2_b)
    w_glu, wbm, wbr, wbs, wo = map(bf, (s5_w_glu, w_branch_mla, w_branch_ret, w_branch_s5, w_o))
    NP = S5_PAIRS

    xf = x.reshape(T, D)
    for l in range(L):
        qc, kvc, kr, rq, rk, rv, rg, u, gates = _in_proj(xf, mod[l], w_packed[l], tm, tps)
        q, k, v = _mla_prep(qc, kvc, kr, cos_m, sin_m, q_gain[l], kv_gain[l],
                            wq[l], wk[l], wv[l], tm)
        o_mla = _mla_attn(q, k, v, B, S, _tile(S, 512), _tile(S, 512))
        o_ret = _retention(rq, rk, rv, rg, cos_r, sin_r, dmask[l], ret_vec[l], B, S, tm)
        y_ssm = _s5_core(u, m_in[l * NP:(l + 1) * NP], t_mat[l * NP:(l + 1) * NP],
                         m_out[l * NP:(l + 1) * NP], a_vec[l], B, S)
        x1 = _merge(xf, mod[l], o_mla, o_ret, y_ssm, u, gates, d_skip[l],
                    w_glu[l], wbm[l], wbr[l], wbs[l], wo[l], g1[l], b1[l], tm, tps, alpha)
        xf = _ffn(x1, mod[l], w_up[l], conv_cw[l], w_down[l], g2[l], b2[l], tm, tps, alpha)
    return xf.reshape(B, S, D)
```

```python
import functools
import math

import jax
import jax.numpy as jnp
import numpy as np
from jax import lax
from jax.experimental import pallas as pl
from jax.experimental.pallas import tpu as pltpu

F32 = jnp.float32
BF16 = jnp.bfloat16
HI = lax.Precision.HIGHEST

MLA_HEADS = 8
MLA_Q_RANK = 256
MLA_KV_RANK = 128
MLA_NOPE = 64
MLA_ROPE = 32
MLA_V = 64
RET_HEADS = 4
RET_DK = 64
RET_DV = 128
RET_CHUNK = 128
S5_GROUP_CH = 16
S5_WIDTH = 512
S5_GROUPS = S5_WIDTH // S5_GROUP_CH
S5_STATE = 64
S5_CHUNK = 16
S5_PAIRS = S5_GROUPS // 2
ROPE_BASE = 10000.0
LN_EPS = 1e-5
RMS_EPS = 1e-6
GN_EPS = 1e-5
FFN_CHUNK = 256

LANES = 128
HEAD_PAD = 128
VMEM_LIMIT = 56 * 1024 * 1024


def _cparams(n_axes):
    return pltpu.CompilerParams(dimension_semantics=("arbitrary",) * n_axes,
                                vmem_limit_bytes=VMEM_LIMIT)


def _const_spec(shape):
    nd = len(shape)
    return pl.BlockSpec(shape, lambda *_: (0,) * nd)


def _layer_norm(x):
    mu = jnp.mean(x, axis=-1, keepdims=True)
    xc = x - mu
    var = jnp.mean(xc * xc, axis=-1, keepdims=True)
    return xc * lax.rsqrt(var + LN_EPS)


def _rms_norm(x, g):
    return x * lax.rsqrt(jnp.mean(x * x, axis=-1, keepdims=True) + RMS_EPS) * g


def _dot(a, b):
    return jnp.dot(a, b, preferred_element_type=F32)


def _dot_nt(a, b):
    return lax.dot_general(a, b, (((1,), (1,)), ((), ())), preferred_element_type=F32)


def _dot_tn(a, b):
    return lax.dot_general(a, b, (((0,), (0,)), ((), ())), preferred_element_type=F32)


def _rope_body(pos_ref, invf_ref, cm_ref, sm_ref, cr_ref, sr_ref):
    p = pos_ref[...].astype(F32)
    ang_m = p * invf_ref[0:1, :]
    ang_r = p * invf_ref[1:2, :]
    cm_ref[...] = jnp.cos(ang_m)
    sm_ref[...] = jnp.sin(ang_m)
    cr_ref[...] = jnp.cos(ang_r)
    sr_ref[...] = jnp.sin(ang_r)


def _rope_tables(positions, tm):
    T = positions.size
    half_m, half_r = MLA_ROPE // 2, RET_DK // 2
    inv_m = ROPE_BASE ** (-jnp.arange(half_m, dtype=F32) / half_m)
    inv_r = ROPE_BASE ** (-jnp.arange(half_r, dtype=F32) / half_r)
    lane = np.arange(LANES)
    row_m = jnp.where((lane >= MLA_NOPE) & (lane < MLA_NOPE + MLA_ROPE),
                      inv_m[(lane - MLA_NOPE) % half_m], 0.0)
    row_r = inv_r[lane % half_r]
    invf = jnp.stack([row_m, row_r]).astype(F32)
    out = jax.ShapeDtypeStruct((T, LANES), F32)
    spec = pl.BlockSpec((tm, LANES), lambda i: (i, 0))
    return pl.pallas_call(
        _rope_body, grid=(T // tm,),
        in_specs=[pl.BlockSpec((tm, 1), lambda i: (i, 0)), _const_spec((2, LANES))],
        out_specs=[spec] * 4, out_shape=[out] * 4,
        compiler_params=_cparams(1), name="rope_tables",
    )(positions.reshape(T, 1), invf)


def _mod_body(c_ref, w_ref, b_ref, o_ref):
    cond = jax.nn.silu(c_ref[...]).astype(BF16)
    o_ref[0] = _dot(cond, w_ref[0].astype(BF16)) + b_ref[0]


def _ada_mod(c, w_ada, b_ada):
    L, D, N = w_ada.shape
    B = c.shape[0]
    tn = D
    return pl.pallas_call(
        _mod_body, grid=(L, N // tn),
        in_specs=[_const_spec((B, D)),
                  pl.BlockSpec((1, D, tn), lambda l, j: (l, 0, j)),
                  pl.BlockSpec((1, 1, tn), lambda l, j: (l, 0, j))],
        out_specs=pl.BlockSpec((1, B, tn), lambda l, j: (l, 0, j)),
        out_shape=jax.ShapeDtypeStruct((L, B, N), F32),
        compiler_params=_cparams(2), name="ada_mod",
    )(c, w_ada, b_ada.reshape(L, 1, N))


_IN_BLOCKS = (("qc", 256, BF16), ("kvc", 128, BF16), ("kr", 128, BF16), ("rq", 256, BF16),
              ("rk", 256, BF16), ("rv", 512, BF16), ("rg", 512, BF16), ("u", 512, F32),
              ("gates", 3072, BF16))


def _in_proj_body(x_ref, mod_ref, w_ref, *out_refs):
    D = x_ref.shape[1]
    xn = _layer_norm(x_ref[...])
    shift = mod_ref[0, :, 0:D]
    scale = mod_ref[0, :, D:2 * D]
    h = (xn * (1.0 + scale) + shift).astype(BF16)
    lo = 0
    for (_, width, _), o_ref in zip(_IN_BLOCKS, out_refs):
        for c0 in range(0, width, 512):
            c1 = min(c0 + 512, width)
            o_ref[:, c0:c1] = _dot(h, w_ref[:, lo + c0:lo + c1]).astype(o_ref.dtype)
        lo += width


def _in_proj(x, mod_l, w_packed, tm, tiles_per_seq):
    T, D = x.shape
    row = lambda i: (i, 0)
    return pl.pallas_call(
        _in_proj_body, grid=(T // tm,),
        in_specs=[pl.BlockSpec((tm, D), row),
                  pl.BlockSpec((1, 1, mod_l.shape[2]), lambda i: (i // tiles_per_seq, 0, 0)),
                  _const_spec(w_packed.shape)],
        out_specs=[pl.BlockSpec((tm, w), row) for _, w, _ in _IN_BLOCKS],
        out_shape=[jax.ShapeDtypeStruct((T, w), dt) for _, w, dt in _IN_BLOCKS],
        compiler_params=_cparams(1), name="in_proj",
    )(x, mod_l, w_packed)


def _rot_half(x, lo, half):
    lane = lax.broadcasted_iota(jnp.int32, x.shape, 1)
    up = pltpu.roll(x, LANES - half, 1)
    down = pltpu.roll(x, half, 1)
    if lo is None:
        return jnp.where(lane % (2 * half) < half, -up, down)
    return jnp.where((lane >= lo) & (lane < lo + half), -up,
                     jnp.where((lane >= lo + half) & (lane < lo + 2 * half), down, 0.0))


def _mla_rope(x, cos, sin):
    tiles = [x[:, i * HEAD_PAD:(i + 1) * HEAD_PAD] for i in range(x.shape[1] // HEAD_PAD)]
    return jnp.concatenate([t * cos + _rot_half(t, MLA_NOPE, MLA_ROPE // 2) * sin for t in tiles],
                           axis=1)


def _mla_prep_body(qc_ref, kvc_ref, kr_ref, cos_ref, sin_ref, qg_ref, kvg_ref,
                   wq_ref, wk_ref, wv_ref, q_ref, k_ref, v_ref):
    cos, sin = cos_ref[...], sin_ref[...]
    qn = _rms_norm(qc_ref[...].astype(F32), qg_ref[...]).astype(BF16)
    scale = (MLA_NOPE + MLA_ROPE) ** -0.5 * math.log2(math.e)
    q_ref[...] = (_mla_rope(_dot(qn, wq_ref[...]), cos, sin) * scale).astype(BF16)
    kvn = _rms_norm(kvc_ref[...].astype(F32), kvg_ref[...]).astype(BF16)
    k_rope = _mla_rope(kr_ref[...].astype(F32), cos, sin)
    k = _dot(kvn, wk_ref[...]) + jnp.concatenate([k_rope] * MLA_HEADS, axis=1)
    k_ref[...] = k.astype(BF16)
    v_ref[...] = _dot(kvn, wv_ref[...]).astype(BF16)


def _mla_prep(qc, kvc, kr, cos_m, sin_m, q_gain, kv_gain, wq, wk, wv, tm):
    T = qc.shape[0]
    row = lambda i: (i, 0)
    qk_w = MLA_HEADS * HEAD_PAD
    v_w = MLA_HEADS * MLA_V
    return pl.pallas_call(
        _mla_prep_body, grid=(T // tm,),
        in_specs=[pl.BlockSpec((tm, qc.shape[1]), row), pl.BlockSpec((tm, kvc.shape[1]), row),
                  pl.BlockSpec((tm, kr.shape[1]), row),
                  pl.BlockSpec((tm, LANES), row), pl.BlockSpec((tm, LANES), row),
                  _const_spec(q_gain.shape), _const_spec(kv_gain.shape),
                  _const_spec(wq.shape), _const_spec(wk.shape), _const_spec(wv.shape)],
        out_specs=[pl.BlockSpec((tm, qk_w), row), pl.BlockSpec((tm, qk_w), row),
                   pl.BlockSpec((tm, v_w), row)],
        out_shape=[jax.ShapeDtypeStruct((T, qk_w), BF16), jax.ShapeDtypeStruct((T, qk_w), BF16),
                   jax.ShapeDtypeStruct((T, v_w), BF16)],
        compiler_params=_cparams(1), name="mla_prep",
    )(qc, kvc, kr, cos_m, sin_m, q_gain, kv_gain, wq, wk, wv)


def _attn_body(q_ref, k_ref, v_ref, o_ref, m_ref, l_ref, acc_ref, *, tk):
    tq = q_ref.shape[0]
    nk = k_ref.shape[0] // tk
    nt = tk // LANES
    heads = [slice(j * HEAD_PAD, (j + 1) * HEAD_PAD) for j in range(2)]
    m_ref[...] = jnp.full(m_ref.shape, -jnp.inf, F32)
    l_ref[...] = jnp.zeros(l_ref.shape, F32)
    acc_ref[...] = jnp.zeros(acc_ref.shape, F32)

    def kstep(t, carry):
        r0 = pl.multiple_of(t * tk, tk)
        v = v_ref[pl.ds(r0, tk), :]
        s = [_dot_nt(q_ref[:, heads[j]], k_ref[pl.ds(r0, tk), heads[j]]) for j in range(2)]
        for j in range(2):
            tiles = [s[j][:, i * LANES:(i + 1) * LANES] for i in range(nt)]
            m_prev = m_ref[j]
            m_new = jnp.maximum(m_prev, jnp.max(functools.reduce(jnp.maximum, tiles),
                                                axis=1, keepdims=True))
            alpha = jnp.exp2(m_prev - m_new)
            p = [jnp.exp2(t_ - m_new) for t_ in tiles]
            l_ref[j] = alpha * l_ref[j] + functools.reduce(jnp.add, p)
            m_ref[j] = m_new
            pb = jnp.concatenate([t_.astype(BF16) for t_ in p], axis=1)
            acc_ref[j] = alpha * acc_ref[j] + _dot(pb, v)
        return carry

    lax.fori_loop(0, nk, kstep, 0, unroll=4)
    outs = [acc_ref[j] / jnp.sum(l_ref[j], axis=1, keepdims=True) for j in range(2)]
    lane = lax.broadcasted_iota(jnp.int32, (tq, 2 * MLA_V), 1)
    o_ref[...] = jnp.where(lane < MLA_V, outs[0], outs[1]).astype(o_ref.dtype)


def _mla_attn(q, k, v, B, S, tq, tk):
    T = q.shape[0]
    nq = S // tq
    pairs = MLA_HEADS // 2
    return pl.pallas_call(
        functools.partial(_attn_body, tk=tk), grid=(B, pairs, nq),
        in_specs=[pl.BlockSpec((tq, 2 * HEAD_PAD), lambda b, h, i: (b * nq + i, h)),
                  pl.BlockSpec((S, 2 * HEAD_PAD), lambda b, h, i: (b, h)),
                  pl.BlockSpec((S, 2 * MLA_V), lambda b, h, i: (b, h))],
        out_specs=pl.BlockSpec((tq, 2 * MLA_V), lambda b, h, i: (b * nq + i, h)),
        out_shape=jax.ShapeDtypeStruct((T, MLA_HEADS * MLA_V), BF16),
        scratch_shapes=[pltpu.VMEM((2, tq, LANES), F32), pltpu.VMEM((2, tq, LANES), F32),
                        pltpu.VMEM((2, tq, 2 * MLA_V), F32)],
        compiler_params=_cparams(3), name="mla_attn",
    )(q, k, v)


_XI_F, _ZETA_F, _XI_B, _ZETA_B, _CARRY_F, _CARRY_B = range(6)


def _ret_rope(r_ref, cos, sin):
    r = r_ref[...].astype(F32)
    tiles = [r[:, i * LANES:(i + 1) * LANES] for i in range(r.shape[1] // LANES)]
    return jnp.concatenate([t * cos + _rot_half(t, None, RET_DK // 2) * sin for t in tiles], axis=1)


def _head_masked(x_pair, h):
    lane = lax.broadcasted_iota(jnp.int32, x_pair.shape, 1)
    lo = (h % 2) * RET_DK
    return jnp.where((lane >= lo) & (lane < lo + RET_DK), x_pair, 0.0)


def _ret_fwd_body(rq_ref, rk_ref, rv_ref, cos_ref, sin_ref, dm_ref, vec_ref, y_ref, s_ref):
    @pl.when(pl.program_id(1) == 0)
    def _():
        s_ref[...] = jnp.zeros(s_ref.shape, F32)

    c = RET_CHUNK
    q = _ret_rope(rq_ref, cos_ref[...], sin_ref[...])
    k = _ret_rope(rk_ref, cos_ref[...], sin_ref[...])
    for ci in range(rq_ref.shape[0] // c):
        rows = slice(ci * c, (ci + 1) * c)
        for h in range(RET_HEADS):
            pair = slice((h // 2) * LANES, (h // 2 + 1) * LANES)
            vcols = slice(h * RET_DV, (h + 1) * RET_DV)
            vec = vec_ref[h]
            qm = _head_masked(q[rows, pair], h)
            kp = k[rows, pair]
            vh = rv_ref[rows, vcols]
            scores = _dot_nt(qm.astype(BF16), kp.astype(BF16)) * dm_ref[h]
            y = _dot(scores.astype(BF16), vh)
            state = s_ref[h]
            y = y + _dot((qm * vec[:, _XI_F:_XI_F + 1]).astype(BF16), state.astype(BF16))
            y_ref[rows, vcols] = y
            kz = (kp * vec[:, _ZETA_F:_ZETA_F + 1]).astype(BF16)
            s_ref[h] = vec[0:1, _CARRY_F:_CARRY_F + 1] * state + _dot_tn(kz, vh)


def _ret_bwd_body(rq_ref, rk_ref, rv_ref, rg_ref, ya_ref, cos_ref, sin_ref, vec_ref, o_ref, s_ref):
    @pl.when(pl.program_id(1) == 0)
    def _():
        s_ref[...] = jnp.zeros(s_ref.shape, F32)

    c = RET_CHUNK
    q = _ret_rope(rq_ref, cos_ref[...], sin_ref[...])
    k = _ret_rope(rk_ref, cos_ref[...], sin_ref[...])
    for ci in reversed(range(rq_ref.shape[0] // c)):
        rows = slice(ci * c, (ci + 1) * c)
        for h in range(RET_HEADS):
            pair = slice((h // 2) * LANES, (h // 2 + 1) * LANES)
            vcols = slice(h * RET_DV, (h + 1) * RET_DV)
            vec = vec_ref[h]
            qm = _head_masked(q[rows, pair], h)
            kp = k[rows, pair]
            vh = rv_ref[rows, vcols]
            state = s_ref[h]
            y = ya_ref[rows, vcols] + _dot((qm * vec[:, _XI_B:_XI_B + 1]).astype(BF16),
                                           state.astype(BF16))
            kz = (kp * vec[:, _ZETA_B:_ZETA_B + 1]).astype(BF16)
            s_ref[h] = vec[0:1, _CARRY_B:_CARRY_B + 1] * state + _dot_tn(kz, vh)
            mu = jnp.mean(y, axis=-1, keepdims=True)
            yc = y - mu
            var = jnp.mean(yc * yc, axis=-1, keepdims=True)
            yn = yc * lax.rsqrt(var + GN_EPS)
            gate = jax.nn.silu(rg_ref[rows, vcols].astype(F32))
            o_ref[rows, vcols] = (gate * yn).astype(o_ref.dtype)


def _retention(rq, rk, rv, rg, cos_r, sin_r, dmask, vec, B, S, tr):
    T = rq.shape[0]
    nb = S // tr
    w = RET_HEADS * RET_DV
    fwd = lambda b, j: (b * nb + j, 0)
    bwd = lambda b, j: (b * nb + nb - 1 - j, 0)
    state = pltpu.VMEM((RET_HEADS, LANES, RET_DV), F32)
    ya = pl.pallas_call(
        _ret_fwd_body, grid=(B, nb),
        in_specs=[pl.BlockSpec((tr, rq.shape[1]), fwd), pl.BlockSpec((tr, rk.shape[1]), fwd),
                  pl.BlockSpec((tr, w), fwd), pl.BlockSpec((tr, LANES), fwd),
                  pl.BlockSpec((tr, LANES), fwd), _const_spec(dmask.shape), _const_spec(vec.shape)],
        out_specs=pl.BlockSpec((tr, w), fwd),
        out_shape=jax.ShapeDtypeStruct((T, w), F32),
        scratch_shapes=[state], compiler_params=_cparams(2), name="ret_fwd",
    )(rq, rk, rv, cos_r, sin_r, dmask, vec)
    return pl.pallas_call(
        _ret_bwd_body, grid=(B, nb),
        in_specs=[pl.BlockSpec((tr, rq.shape[1]), bwd), pl.BlockSpec((tr, rk.shape[1]), bwd),
                  pl.BlockSpec((tr, w), bwd), pl.BlockSpec((tr, w), bwd), pl.BlockSpec((tr, w), bwd),
                  pl.BlockSpec((tr, LANES), bwd), pl.BlockSpec((tr, LANES), bwd),
                  _const_spec(vec.shape)],
        out_specs=pl.BlockSpec((tr, w), bwd),
        out_shape=jax.ShapeDtypeStruct((T, w), BF16),
        scratch_shapes=[state], compiler_params=_cparams(2), name="ret_bwd",
    )(rq, rk, rv, rg, ya, cos_r, sin_r, vec)


PAIR_W = 2 * S5_GROUP_CH
PAIRS_PER_TILE = LANES // PAIR_W
CHUNK_W = S5_CHUNK * PAIR_W


def _chunk_rows(u_ref, n):
    tok = [u_ref[pl.ds(i, n, stride=S5_CHUNK), :] for i in range(S5_CHUNK)]
    return [jnp.concatenate([t[:, p * PAIR_W:(p + 1) * PAIR_W] for t in tok], axis=1)
            for p in range(PAIRS_PER_TILE)]


def _s5_in_body(u_ref, min_ref, fr_ref, fi_ref, br_ref, bi_ref):
    n = fr_ref.shape[1]
    for p, rows in enumerate(_chunk_rows(u_ref, n)):
        v = _dot_nt(rows.astype(BF16), min_ref[p])
        for part, ref in enumerate((fr_ref, fi_ref, br_ref, bi_ref)):
            ref[0, :, p * LANES:(p + 1) * LANES] = v[:, part * LANES:(part + 1) * LANES]


def _s5_scan_body(fr_ref, fi_ref, br_ref, bi_ref, a_ref, zfr_ref, zfi_ref, zbr_ref, zbi_ref, *, rb):
    n = fr_ref.shape[1]
    nblk = n // rb
    lw = fr_ref.shape[2]
    afr, afi = a_ref[0:1, :], a_ref[1:2, :]
    abr, abi = a_ref[2:3, :], a_ref[3:4, :]
    zero = jnp.zeros((1, lw), F32)

    def step(blk, carry):
        xr, xi, yr, yi = carry
        r0 = pl.multiple_of(blk * rb, rb)
        vr, vi = fr_ref[0, pl.ds(r0, rb), :], fi_ref[0, pl.ds(r0, rb), :]
        out_r, out_i = [], []
        for r in range(rb):
            out_r.append(xr)
            out_i.append(xi)
            xr, xi = (afr * xr - afi * xi + vr[r:r + 1], afr * xi + afi * xr + vi[r:r + 1])
        zfr_ref[0, pl.ds(r0, rb), :] = jnp.concatenate(out_r, axis=0).astype(zfr_ref.dtype)
        zfi_ref[0, pl.ds(r0, rb), :] = jnp.concatenate(out_i, axis=0).astype(zfi_ref.dtype)
        r1 = pl.multiple_of((nblk - 1 - blk) * rb, rb)
        wr, wi = br_ref[0, pl.ds(r1, rb), :], bi_ref[0, pl.ds(r1, rb), :]
        out_r, out_i = [None] * rb, [None] * rb
        for r in reversed(range(rb)):
            out_r[r] = yr
            out_i[r] = yi
            yr, yi = (abr * yr - abi * yi + wr[r:r + 1], abr * yi + abi * yr + wi[r:r + 1])
        zbr_ref[0, pl.ds(r1, rb), :] = jnp.concatenate(out_r, axis=0).astype(zbr_ref.dtype)
        zbi_ref[0, pl.ds(r1, rb), :] = jnp.concatenate(out_i, axis=0).astype(zbi_ref.dtype)
        return xr, xi, yr, yi

    lax.fori_loop(0, nblk, step, (zero, zero, zero, zero))


def _s5_out_body(u_ref, zfr_ref, zfi_ref, zbr_ref, zbi_ref, t_ref, mout_ref, y_ref):
    n = zfr_ref.shape[1]
    ys = []
    for p, rows in enumerate(_chunk_rows(u_ref, n)):
        cols = slice(p * LANES, (p + 1) * LANES)
        z = jnp.concatenate([r[0, :, cols] for r in (zfr_ref, zfi_ref, zbr_ref, zbi_ref)], axis=1)
        ys.append(_dot(rows.astype(BF16), t_ref[p]) + _dot(z, mout_ref[p]))
    for i in range(S5_CHUNK):
        y_ref[pl.ds(i, n, stride=S5_CHUNK), :] = jnp.concatenate(
            [y[:, i * PAIR_W:(i + 1) * PAIR_W] for y in ys], axis=1)


def _s5_core(u, m_in, t_mat, m_out, a_vec, B, S):
    N = S // S5_CHUNK
    n_tiles = S5_WIDTH // LANES
    sw = S5_GROUPS * S5_STATE
    pw = PAIRS_PER_TILE * LANES
    u_spec = pl.BlockSpec((S, LANES), lambda b, q: (b, q))
    part_spec = pl.BlockSpec((1, N, pw), lambda b, q: (b, 0, q))
    mat_spec = pl.BlockSpec((PAIRS_PER_TILE, CHUNK_W, CHUNK_W), lambda b, q: (q, 0, 0))
    parts = pl.pallas_call(
        _s5_in_body, grid=(B, n_tiles),
        in_specs=[u_spec, mat_spec], out_specs=[part_spec] * 4,
        out_shape=[jax.ShapeDtypeStruct((B, N, sw), F32)] * 4,
        compiler_params=_cparams(2), name="s5_in",
    )(u, m_in)
    lw = 512
    blk = pl.BlockSpec((1, N, lw), lambda b, j: (b, 0, j))
    z = pl.pallas_call(
        functools.partial(_s5_scan_body, rb=16), grid=(B, sw // lw),
        in_specs=[blk] * 4 + [pl.BlockSpec((4, lw), lambda b, j: (0, j))],
        out_specs=[blk] * 4, out_shape=[jax.ShapeDtypeStruct((B, N, sw), BF16)] * 4,
        compiler_params=_cparams(2), name="s5_scan",
    )(*parts, a_vec)
    return pl.pallas_call(
        _s5_out_body, grid=(B, n_tiles),
        in_specs=[u_spec] + [part_spec] * 4 + [mat_spec, mat_spec],
        out_specs=u_spec, out_shape=jax.ShapeDtypeStruct((B * S, S5_WIDTH), F32),
        compiler_params=_cparams(2), name="s5_out",
    )(u, *z, t_mat, m_out)


def _merge_body(x_ref, mod_ref, om_ref, or_ref, ys_ref, u_ref, g_ref, d_ref, wglu_ref,
                wm_ref, wr_ref, ws_ref, wo_ref, lng_ref, lnb_ref, o_ref, *, alpha):
    D = x_ref.shape[1]
    y = d_ref[...] * u_ref[...].astype(F32) + ys_ref[...].astype(F32)
    y = jax.nn.gelu(y)
    o_s5 = y * jax.nn.sigmoid(_dot(y.astype(BF16), wglu_ref[...]))
    merged = (jax.nn.sigmoid(g_ref[:, 0:D].astype(F32)) * _dot(om_ref[...], wm_ref[...])
              + jax.nn.sigmoid(g_ref[:, D:2 * D].astype(F32)) * _dot(or_ref[...], wr_ref[...])
              + jax.nn.sigmoid(g_ref[:, 2 * D:3 * D].astype(F32)) * _dot(o_s5.astype(BF16), ws_ref[...]))
    out = _dot(merged.astype(BF16), wo_ref[...])
    gate1 = mod_ref[0, :, 2 * D:3 * D]
    o_ref[...] = _layer_norm(alpha * x_ref[...] + gate1 * out) * lng_ref[...] + lnb_ref[...]


def _merge(x, mod_l, o_mla, o_ret, y_ssm, u, gates, d_skip, w_glu, wm, wr, ws, wo, ln_g, ln_b,
           tm, tiles_per_seq, alpha):
    T, D = x.shape
    row = lambda i: (i, 0)
    consts = (d_skip, w_glu, wm, wr, ws, wo, ln_g, ln_b)
    return pl.pallas_call(
        functools.partial(_merge_body, alpha=alpha), grid=(T // tm,),
        in_specs=[pl.BlockSpec((tm, D), row),
                  pl.BlockSpec((1, 1, mod_l.shape[2]), lambda i: (i // tiles_per_seq, 0, 0))]
                 + [pl.BlockSpec((tm, a.shape[1]), row) for a in (o_mla, o_ret, y_ssm, u, gates)]
                 + [_const_spec(a.shape) for a in consts],
        out_specs=pl.BlockSpec((tm, D), row),
        out_shape=jax.ShapeDtypeStruct((T, D), F32),
        compiler_params=_cparams(1), name="merge",
    )(x, mod_l, o_mla, o_ret, y_ssm, u, gates, *consts)


HALO = 8


def _ffn_body(x_ref, xp_ref, xn_ref, mod_ref, wup_ref, cw_ref, wd_ref,
              lng_ref, lnb_ref, o_ref, act_ref, *, alpha, tiles_per_seq):
    tm, D = x_ref.shape
    dff = wd_ref.shape[0]
    i = pl.program_id(0)
    shift = mod_ref[0, :, 3 * D:4 * D]
    scale = mod_ref[0, :, 4 * D:5 * D]
    gate2 = mod_ref[0, :, 5 * D:6 * D]
    x = x_ref[...]
    x_ext = jnp.concatenate([xp_ref[...], x, xn_ref[...]], axis=0)
    h = _layer_norm(x_ext) * (1.0 + scale) + shift
    r = lax.broadcasted_iota(jnp.int32, (tm + 2 * HALO, 1), 0)
    pos_in_seq = i % tiles_per_seq
    lo = jnp.where(pos_in_seq == 0, HALO, 0)
    hi = jnp.where(pos_in_seq == tiles_per_seq - 1, tm + HALO, tm + 2 * HALO)
    h = jnp.where((r >= lo) & (r < hi), h, 0.0).astype(BF16)
    n_ext = tm + 2 * HALO

    def conv_up(c0):
        cols = slice(c0, c0 + FFN_CHUNK)
        up = _dot(h, wup_ref[:, cols])
        prev = pltpu.roll(up, 1, 0)
        nxt = pltpu.roll(up, n_ext - 1, 0)
        y = (prev * cw_ref[0:1, cols] + up * cw_ref[1:2, cols] + nxt * cw_ref[2:3, cols]
             + cw_ref[3:4, cols])
        return y[HALO:HALO + tm, :]

    for c0 in range(0, dff, FFN_CHUNK):
        act_ref[:, c0:c0 + FFN_CHUNK] = (jax.nn.silu(conv_up(dff + c0)) * conv_up(c0)).astype(BF16)
    f = _dot(act_ref[...], wd_ref[...])
    o_ref[...] = _layer_norm(alpha * x + gate2 * f) * lng_ref[...] + lnb_ref[...]


def _ffn(x, mod_l, w_up, cw, wd, ln_g, ln_b, tm, tiles_per_seq, alpha):
    T, D = x.shape
    hb = tm // HALO
    n_hb = T // HALO
    consts = (w_up, cw, wd, ln_g, ln_b)
    return pl.pallas_call(
        functools.partial(_ffn_body, alpha=alpha, tiles_per_seq=tiles_per_seq), grid=(T // tm,),
        in_specs=[pl.BlockSpec((tm, D), lambda i: (i, 0)),
                  pl.BlockSpec((HALO, D), lambda i: (jnp.maximum(i * hb - 1, 0), 0)),
                  pl.BlockSpec((HALO, D), lambda i: (jnp.minimum((i + 1) * hb, n_hb - 1), 0)),
                  pl.BlockSpec((1, 1, mod_l.shape[2]), lambda i: (i // tiles_per_seq, 0, 0))]
                 + [_const_spec(a.shape) for a in consts],
        out_specs=pl.BlockSpec((tm, D), lambda i: (i, 0)),
        out_shape=jax.ShapeDtypeStruct((T, D), F32),
        scratch_shapes=[pltpu.VMEM((tm, wd.shape[0]), BF16)],
        compiler_params=_cparams(1), name="ffn",
    )(x, x, x, mod_l, *consts)


def _pack_w_in(w_in):
    widths = [MLA_Q_RANK, MLA_KV_RANK, MLA_ROPE, RET_HEADS * RET_DK, RET_HEADS * RET_DK,
              RET_HEADS * RET_DV, RET_HEADS * RET_DV, S5_WIDTH]
    sp = [0] + [int(v) for v in np.cumsum(widths)]
    w_qc, w_kvc, w_kr, w_rq, w_rk, w_rv, w_rg, w_u = [w_in[..., sp[i]:sp[i + 1]] for i in range(8)]
    w_gates = w_in[..., sp[8]:]
    w_kr = jnp.pad(w_kr, ((0, 0), (0, 0), (MLA_NOPE, HEAD_PAD - MLA_NOPE - MLA_ROPE)))
    blocks = [w_qc, w_kvc, w_kr, w_rq * (RET_DK ** -0.5), w_rk, w_rv, w_rg, w_u, w_gates]
    return jnp.concatenate(blocks, axis=-1).astype(BF16)


def _pack_mla(w_uq, w_ukv):
    L = w_uq.shape[0]
    dq = MLA_NOPE + MLA_ROPE
    wq = w_uq.reshape(L, MLA_Q_RANK, MLA_HEADS, dq)
    wq = jnp.pad(wq, ((0, 0), (0, 0), (0, 0), (0, HEAD_PAD - dq)))
    wkv = w_ukv.reshape(L, MLA_KV_RANK, MLA_HEADS, MLA_NOPE + MLA_V)
    wk = jnp.pad(wkv[..., :MLA_NOPE], ((0, 0), (0, 0), (0, 0), (0, HEAD_PAD - MLA_NOPE)))
    wv = wkv[..., MLA_NOPE:].reshape(L, MLA_KV_RANK, MLA_HEADS * MLA_V)
    flat = lambda w: w.reshape(L, w.shape[1], MLA_HEADS * HEAD_PAD).astype(BF16)
    return flat(wq), flat(wk), wv.astype(BF16)


def _ret_tables(ret_log_decay):
    c = RET_CHUNK
    lg = jnp.log1p(-jnp.exp(ret_log_decay.astype(F32)))
    lg_f, lg_b = lg[:, 0, :, None, None], lg[:, 1, :, None, None]
    idx = jnp.arange(c, dtype=F32)
    diff = idx[:, None] - idx[None, :]
    dmask = jnp.where(diff >= 0, jnp.exp(lg_f * jnp.maximum(diff, 0.0)),
                      jnp.exp(lg_b * jnp.maximum(-diff, 0.0)))
    lf, lb = lg[:, 0, :, None], lg[:, 1, :, None]
    ones = jnp.ones((c,), F32)
    cols = [jnp.exp(lf * (idx + 1)), jnp.exp(lf * (c - 1 - idx)), jnp.exp(lb * (c - idx)),
            jnp.exp(lb * idx), jnp.exp(lf * c) * ones, jnp.exp(lb * c) * ones,
            0.0 * lf * ones, 0.0 * lf * ones]
    return dmask, jnp.stack(cols, axis=-1)


def _s5_tables(lam_re, lam_im, log_step, b_re, b_im, c_re, c_im):
    L = lam_re.shape[0]
    G, P, CH, C = S5_GROUPS, S5_STATE, S5_GROUP_CH, S5_CHUNK
    f32 = lambda t: t.astype(F32)
    lam_re, lam_im, b_re, b_im, c_re, c_im = map(f32, (lam_re, lam_im, b_re, b_im, c_re, c_im))
    step = jnp.exp(f32(log_step))[..., None]
    mag = jnp.exp(lam_re * step)
    a_re, a_im = mag * jnp.cos(lam_im * step), mag * jnp.sin(lam_im * step)
    den = jnp.square(lam_re) + jnp.square(lam_im)
    f_re = ((a_re - 1.0) * lam_re + a_im * lam_im) / den
    f_im = (a_im * lam_re - (a_re - 1.0) * lam_im) / den
    bb_re = f_re[..., None] * b_re - f_im[..., None] * b_im
    bb_im = f_re[..., None] * b_im + f_im[..., None] * b_re
    k = jnp.arange(C + 1, dtype=F32)[:, None, None, None, None]
    pm = jnp.exp(k * (lam_re * step))
    pw_re, pw_im = pm * jnp.cos(k * (lam_im * step)), pm * jnp.sin(k * (lam_im * step))

    NP = S5_PAIRS
    lane = np.arange(CHUNK_W)
    tok, grp, ch = lane // PAIR_W, (lane // CH) % 2, lane % CH
    onehot = lambda idx, n: jnp.asarray(np.arange(n)[:, None] == idx[None, :], F32)
    spread = lambda t, sel: jnp.einsum('...k,kn->...n', t, sel, precision=HI)
    ch_sel = onehot(ch, CH)
    own_lanes = jnp.asarray(grp[None, :] == (np.arange(G) % 2)[:, None], F32)[:, None, :]

    def times(z, mat_re, mat_im, expo):
        sel = onehot(expo, C + 1)
        pr = spread(jnp.moveaxis(pw_re[:, :, z], 0, -1), sel) * own_lanes
        pi = spread(jnp.moveaxis(pw_im[:, :, z], 0, -1), sel) * own_lanes
        mr, mi = spread(mat_re[:, z], ch_sel), spread(mat_im[:, z], ch_sel)
        return mr * pr - mi * pi, mr * pi + mi * pr

    cr, ci = jnp.swapaxes(c_re, -1, -2), jnp.swapaxes(c_im, -1, -2)
    pairs = lambda t: t.reshape((L, NP, 2) + t.shape[2:])

    of_re, of_im = times(0, cr, ci, tok + 1)
    ob_re, ob_im = times(1, cr, ci, C - tok)
    m_out = jnp.stack([pairs(t) for t in (of_re, -of_im, ob_re, -ob_im)], axis=2)
    m_out = m_out.reshape(L * NP, 4 * 2 * P, CHUNK_W)
    if_re, if_im = times(0, bb_re, bb_im, C - 1 - tok)
    ib_re, ib_im = times(1, bb_re, bb_im, tok)
    m_in_t = jnp.stack([pairs(t) for t in (if_re, if_im, ib_re, ib_im)], axis=2)
    m_in_t = m_in_t.reshape(L * NP, 4 * 2 * P, CHUNK_W)

    def lag_row(z, expo):
        h_re, h_im = times(z, cr, ci, expo)
        bt_re, bt_im = jnp.swapaxes(bb_re[:, z], -1, -2), jnp.swapaxes(bb_im[:, z], -1, -2)
        return (jnp.einsum('lgcp,lgpn->lgcn', bt_re, h_re, precision=HI)
                - jnp.einsum('lgcp,lgpn->lgcn', bt_im, h_im, precision=HI))

    row_f, row_b = lag_row(0, tok), lag_row(1, C - 1 - tok)
    lane_i = jnp.asarray(lane)
    blocks = []
    for j in range(C):
        fwd = jnp.where(lane_i >= j * PAIR_W, jnp.roll(row_f, j * PAIR_W, axis=-1), 0.0)
        bwd = jnp.where(lane_i < (j + 1) * PAIR_W, jnp.roll(row_b, -(C - 1 - j) * PAIR_W, axis=-1), 0.0)
        blocks.append(pairs(fwd + bwd))
    t_p = jnp.stack(blocks, axis=2).reshape(L * NP, CHUNK_W, CHUNK_W)

    a_vec = jnp.stack([pw_re[C, :, 0], pw_im[C, :, 0], pw_re[C, :, 1], pw_im[C, :, 1]], axis=1)
    a_vec = a_vec.reshape(L, 4, G * P)
    return t_p.astype(BF16), m_in_t.astype(BF16), m_out.astype(BF16), a_vec


def _pack_ffn(w_up, conv_w, conv_b, w_down):
    cw = jnp.concatenate([conv_w, conv_b[:, None, :]], axis=1)
    return w_up.astype(BF16), cw, w_down.astype(BF16)


def _tile(n, pref):
    t = min(pref, n)
    assert n % t == 0, (n, t)
    return t


def kernel(x, c, positions, w_in, mla_q_norm, mla_w_uq, mla_kv_norm, mla_w_ukv, ret_log_decay, s5_lam_re, s5_lam_im, s5_log_step, s5_b_re, s5_b_im, s5_c_re, s5_c_im, s5_d, s5_w_glu, w_branch_mla, w_branch_ret, w_branch_s5, w_o, ffn_w_up, ffn_conv_w, ffn_conv_b, ffn_w_down, ln1_g, ln1_b, ln2_g, ln2_b, w_ada, b_ada):
    B, S, D = x.shape
    L = w_in.shape[0]
    T = B * S
    assert S % RET_CHUNK == 0 and S % (16 * S5_CHUNK) == 0 and ffn_w_up.shape[2] % (2 * FFN_CHUNK) == 0
    alpha = (2 * L) ** 0.25
    tm = _tile(S, 512)
    tps = S // tm

    cos_m, sin_m, cos_r, sin_r = _rope_tables(positions, _tile(T, 2048))
    mod = _ada_mod(c, w_ada, b_ada).reshape(L, B, 1, -1)

    w_packed = _pack_w_in(w_in)
    wq, wk, wv = _pack_mla(mla_w_uq, mla_w_ukv)
    dmask, ret_vec = _ret_tables(ret_log_decay)
    t_mat, m_in, m_out, a_vec = _s5_tables(s5_lam_re, s5_lam_im, s5_log_step, s5_b_re, s5_b_im,
                                           s5_c_re, s5_c_im)
    w_up, conv_cw, w_down = _pack_ffn(ffn_w_up, ffn_conv_w, ffn_conv_b, ffn_w_down)
    bf = lambda w: w.astype(BF16)
    row2 = lambda v: v.reshape(L, 1, -1)
    q_gain, kv_gain = row2(mla_q_norm), row2(mla_kv_norm)
    d_skip = row2(s5_d)
    g1, b1, g2, b2 = row2(ln1_g), row2(ln1_b), row2(ln2_g), row2(ln2_b)
    w_glu, wbm, wbr, wbs, wo = map(bf, (s5_w_glu, w_branch_mla, w_branch_ret, w_branch_s5, w_o))
    NP = S5_PAIRS

    xf = x.reshape(T, D)
    for l in range(L):
        qc, kvc, kr, rq, rk, rv, rg, u, gates = _in_proj(xf, mod[l], w_packed[l], tm, tps)
        q, k, v = _mla_prep(qc, kvc, kr, cos_m, sin_m, q_gain[l], kv_gain[l],
                            wq[l], wk[l], wv[l], tm)
        o_mla = _mla_attn(q, k, v, B, S, _tile(S, 1024), _tile(S, 512))
        o_ret = _retention(rq, rk, rv, rg, cos_r, sin_r, dmask[l], ret_vec[l], B, S, tm)
        y_ssm = _s5_core(u, m_in[l * NP:(l + 1) * NP], t_mat[l * NP:(l + 1) * NP],
                         m_out[l * NP:(l + 1) * NP], a_vec[l], B, S)
        x1 = _merge(xf, mod[l], o_mla, o_ret, y_ssm, u, gates, d_skip[l],
                    w_glu[l], wbm[l], wbr[l], wbs[l], wo[l], g1[l], b1[l], tm, tps, alpha)
        xf = _ffn(x1, mod[l], w_up[l], conv_cw[l], w_down[l], g2[l], b2[l], tm, tps, alpha)
    return xf.reshape(B, S, D)
```

```python
import functools
import math

import jax
import jax.numpy as jnp
import numpy as np
from jax import lax
from jax.experimental import pallas as pl
from jax.experimental.pallas import tpu as pltpu

F32 = jnp.float32
BF16 = jnp.bfloat16
HI = lax.Precision.HIGHEST

MLA_HEADS = 8
MLA_Q_RANK = 256
MLA_KV_RANK = 128
MLA_NOPE = 64
MLA_ROPE = 32
MLA_V = 64
RET_HEADS = 4
RET_DK = 64
RET_DV = 128
S5_GROUP_CH = 16
S5_WIDTH = 512
S5_GROUPS = S5_WIDTH // S5_GROUP_CH
S5_STATE = 64
S5_CHUNK = 16
S5_PAIRS = S5_GROUPS // 2
ROPE_BASE = 10000.0
LN_EPS = 1e-5
RMS_EPS = 1e-6
GN_EPS = 1e-5
FFN_CHUNK = 256

LANES = 128
HEAD_PAD = 128
VMEM_LIMIT = 56 * 1024 * 1024


def _cparams(n_axes):
    return pltpu.CompilerParams(dimension_semantics=("arbitrary",) * n_axes,
                                vmem_limit_bytes=VMEM_LIMIT)


def _const_spec(shape):
    nd = len(shape)
    return pl.BlockSpec(shape, lambda *_: (0,) * nd)


def _layer_norm(x):
    mu = jnp.mean(x, axis=-1, keepdims=True)
    xc = x - mu
    var = jnp.mean(xc * xc, axis=-1, keepdims=True)
    return xc * lax.rsqrt(var + LN_EPS)


def _rms_norm(x, g):
    return x * lax.rsqrt(jnp.mean(x * x, axis=-1, keepdims=True) + RMS_EPS) * g


def _dot(a, b):
    return jnp.dot(a, b, preferred_element_type=F32)


def _dot_nt(a, b):
    return lax.dot_general(a, b, (((1,), (1,)), ((), ())), preferred_element_type=F32)


def _transposed(a):
    n = a.shape[1]
    eye = (lax.broadcasted_iota(jnp.int32, (n, n), 0)
           == lax.broadcasted_iota(jnp.int32, (n, n), 1)).astype(a.dtype)
    return _dot_nt(eye, a).astype(a.dtype)


def _rope_body(pos_ref, invf_ref, cm_ref, sm_ref, cr_ref, sr_ref):
    p = pos_ref[...].astype(F32)
    ang_m = p * invf_ref[0:1, :]
    ang_r = p * invf_ref[1:2, :]
    cm_ref[...] = jnp.cos(ang_m)
    sm_ref[...] = jnp.sin(ang_m)
    cr_ref[...] = jnp.cos(ang_r)
    sr_ref[...] = jnp.sin(ang_r)


def _rope_tables(positions, tm):
    T = positions.size
    half_m, half_r = MLA_ROPE // 2, RET_DK // 2
    inv_m = ROPE_BASE ** (-jnp.arange(half_m, dtype=F32) / half_m)
    inv_r = ROPE_BASE ** (-jnp.arange(half_r, dtype=F32) / half_r)
    lane = np.arange(LANES)
    row_m = jnp.where((lane >= MLA_NOPE) & (lane < MLA_NOPE + MLA_ROPE),
                      inv_m[(lane - MLA_NOPE) % half_m], 0.0)
    row_r = inv_r[lane % half_r]
    invf = jnp.stack([row_m, row_r]).astype(F32)
    out = jax.ShapeDtypeStruct((T, LANES), F32)
    spec = pl.BlockSpec((tm, LANES), lambda i: (i, 0))
    return pl.pallas_call(
        _rope_body, grid=(T // tm,),
        in_specs=[pl.BlockSpec((tm, 1), lambda i: (i, 0)), _const_spec((2, LANES))],
        out_specs=[spec] * 4, out_shape=[out] * 4,
        compiler_params=_cparams(1), name="rope_tables",
    )(positions.reshape(T, 1), invf)


def _mod_body(c_ref, w_ref, b_ref, o_ref):
    cond = jax.nn.silu(c_ref[...]).astype(BF16)
    o_ref[0] = _dot(cond, w_ref[0].astype(BF16)) + b_ref[0]


def _ada_mod(c, w_ada, b_ada):
    L, D, N = w_ada.shape
    B = c.shape[0]
    tn = D
    return pl.pallas_call(
        _mod_body, grid=(L, N // tn),
        in_specs=[_const_spec((B, D)),
                  pl.BlockSpec((1, D, tn), lambda l, j: (l, 0, j)),
                  pl.BlockSpec((1, 1, tn), lambda l, j: (l, 0, j))],
        out_specs=pl.BlockSpec((1, B, tn), lambda l, j: (l, 0, j)),
        out_shape=jax.ShapeDtypeStruct((L, B, N), F32),
        compiler_params=_cparams(2), name="ada_mod",
    )(c, w_ada, b_ada.reshape(L, 1, N))


_MLA_IN = MLA_Q_RANK + MLA_KV_RANK + HEAD_PAD
_IN_BLOCKS = (("rq", 256, BF16), ("rk", 256, BF16), ("rv", 512, BF16), ("rg", 512, BF16),
              ("u", 512, F32), ("gates", 3072, BF16))


def _in_proj_body(x_ref, mod_ref, w_ref, cos_ref, sin_ref, qg_ref, kvg_ref, wq_ref, wk_ref, wv_ref,
                  q_ref, k_ref, v_ref, *out_refs):
    D = x_ref.shape[1]
    xn = _layer_norm(x_ref[...])
    shift = mod_ref[0, :, 0:D]
    scale = mod_ref[0, :, D:2 * D]
    h = (xn * (1.0 + scale) + shift).astype(BF16)

    lat = _dot(h, w_ref[:, 0:_MLA_IN])
    cos, sin = cos_ref[...], sin_ref[...]
    qn = _rms_norm(lat[:, :MLA_Q_RANK], qg_ref[...]).astype(BF16)
    q_scale = (MLA_NOPE + MLA_ROPE) ** -0.5 * math.log2(math.e)
    q_ref[...] = (_mla_rope(_dot(qn, wq_ref[...]), cos, sin) * q_scale).astype(BF16)
    kvn = _rms_norm(lat[:, MLA_Q_RANK:MLA_Q_RANK + MLA_KV_RANK], kvg_ref[...]).astype(BF16)
    k_rope = _mla_rope(lat[:, MLA_Q_RANK + MLA_KV_RANK:], cos, sin)
    k = _dot(kvn, wk_ref[...]) + jnp.concatenate([k_rope] * MLA_HEADS, axis=1)
    k_ref[...] = k.astype(BF16)
    v_ref[...] = _dot(kvn, wv_ref[...]).astype(BF16)

    lo = _MLA_IN
    for (_, width, _), o_ref in zip(_IN_BLOCKS, out_refs):
        for c0 in range(0, width, 512):
            c1 = min(c0 + 512, width)
            o_ref[:, c0:c1] = _dot(h, w_ref[:, lo + c0:lo + c1]).astype(o_ref.dtype)
        lo += width


def _in_proj(x, mod_l, w_packed, cos_m, sin_m, q_gain, kv_gain, wq, wk, wv, tm, tiles_per_seq):
    T, D = x.shape
    row = lambda i: (i, 0)
    consts = (q_gain, kv_gain, wq, wk, wv)
    mla_out = [(MLA_HEADS * HEAD_PAD, BF16), (MLA_HEADS * HEAD_PAD, BF16), (MLA_HEADS * MLA_V, BF16)]
    outs = mla_out + [(w, dt) for _, w, dt in _IN_BLOCKS]
    return pl.pallas_call(
        _in_proj_body, grid=(T // tm,),
        in_specs=[pl.BlockSpec((tm, D), row),
                  pl.BlockSpec((1, 1, mod_l.shape[2]), lambda i: (i // tiles_per_seq, 0, 0)),
                  _const_spec(w_packed.shape),
                  pl.BlockSpec((tm, LANES), row), pl.BlockSpec((tm, LANES), row)]
                 + [_const_spec(a.shape) for a in consts],
        out_specs=[pl.BlockSpec((tm, w), row) for w, _ in outs],
        out_shape=[jax.ShapeDtypeStruct((T, w), dt) for w, dt in outs],
        compiler_params=_cparams(1), name="in_proj",
    )(x, mod_l, w_packed, cos_m, sin_m, *consts)


def _rot_half(x, lo, half):
    lane = lax.broadcasted_iota(jnp.int32, x.shape, 1)
    up = pltpu.roll(x, LANES - half, 1)
    down = pltpu.roll(x, half, 1)
    if lo is None:
        return jnp.where(lane % (2 * half) < half, -up, down)
    return jnp.where((lane >= lo) & (lane < lo + half), -up,
                     jnp.where((lane >= lo + half) & (lane < lo + 2 * half), down, 0.0))


def _mla_rope(x, cos, sin):
    tiles = [x[:, i * HEAD_PAD:(i + 1) * HEAD_PAD] for i in range(x.shape[1] // HEAD_PAD)]
    return jnp.concatenate([t * cos + _rot_half(t, MLA_NOPE, MLA_ROPE // 2) * sin for t in tiles],
                           axis=1)


def _attn_body(q_ref, k_ref, v_ref, o_ref, m_ref, l_ref, acc_ref, *, tk):
    tq = q_ref.shape[0]
    nk = k_ref.shape[0] // tk
    nt = tk // LANES
    heads = [slice(j * HEAD_PAD, (j + 1) * HEAD_PAD) for j in range(2)]
    m_ref[...] = jnp.full(m_ref.shape, -jnp.inf, F32)
    l_ref[...] = jnp.zeros(l_ref.shape, F32)
    acc_ref[...] = jnp.zeros(acc_ref.shape, F32)

    def kstep(t, carry):
        r0 = pl.multiple_of(t * tk, tk)
        v = v_ref[pl.ds(r0, tk), :]
        s = [_dot_nt(q_ref[:, heads[j]], k_ref[pl.ds(r0, tk), heads[j]]) for j in range(2)]
        for j in range(2):
            tiles = [s[j][:, i * LANES:(i + 1) * LANES] for i in range(nt)]
            m_prev = m_ref[j]
            m_new = jnp.maximum(m_prev, jnp.max(functools.reduce(jnp.maximum, tiles),
                                                axis=1, keepdims=True))
            alpha = jnp.exp2(m_prev - m_new)
            p = [jnp.exp2(t_ - m_new) for t_ in tiles]
            l_ref[j] = alpha * l_ref[j] + functools.reduce(jnp.add, p)
            m_ref[j] = m_new
            pb = jnp.concatenate([t_.astype(BF16) for t_ in p], axis=1)
            acc_ref[j] = alpha * acc_ref[j] + _dot(pb, v)
        return carry

    lax.fori_loop(0, nk, kstep, 0, unroll=4)
    outs = [acc_ref[j] / jnp.sum(l_ref[j], axis=1, keepdims=True) for j in range(2)]
    lane = lax.broadcasted_iota(jnp.int32, (tq, 2 * MLA_V), 1)
    o_ref[...] = jnp.where(lane < MLA_V, outs[0], outs[1]).astype(o_ref.dtype)


def _mla_attn(q, k, v, B, S, tq, tk):
    T = q.shape[0]
    nq = S // tq
    pairs = MLA_HEADS // 2
    return pl.pallas_call(
        functools.partial(_attn_body, tk=tk), grid=(B, pairs, nq),
        in_specs=[pl.BlockSpec((tq, 2 * HEAD_PAD), lambda b, h, i: (b * nq + i, h)),
                  pl.BlockSpec((S, 2 * HEAD_PAD), lambda b, h, i: (b, h)),
                  pl.BlockSpec((S, 2 * MLA_V), lambda b, h, i: (b, h))],
        out_specs=pl.BlockSpec((tq, 2 * MLA_V), lambda b, h, i: (b * nq + i, h)),
        out_shape=jax.ShapeDtypeStruct((T, MLA_HEADS * MLA_V), BF16),
        scratch_shapes=[pltpu.VMEM((2, tq, LANES), F32), pltpu.VMEM((2, tq, LANES), F32),
                        pltpu.VMEM((2, tq, 2 * MLA_V), F32)],
        compiler_params=_cparams(3), name="mla_attn",
    )(q, k, v)


_XI_F, _ZETA_F, _XI_B, _ZETA_B, _CARRY_F, _CARRY_B = range(6)


def _ret_rope(r_ref, cos, sin):
    r = r_ref[...].astype(F32)
    tiles = [r[:, i * LANES:(i + 1) * LANES] for i in range(r.shape[1] // LANES)]
    return jnp.concatenate([t * cos + _rot_half(t, None, RET_DK // 2) * sin for t in tiles], axis=1)


def _head_masked(x_pair, h):
    lane = lax.broadcasted_iota(jnp.int32, x_pair.shape, 1)
    lo = (h % 2) * RET_DK
    return jnp.where((lane >= lo) & (lane < lo + RET_DK), x_pair, 0.0)


def _ret_fwd_body(rq_ref, rk_ref, rv_ref, cos_ref, sin_ref, dm_ref, vec_ref, y_ref, s_ref):
    @pl.when(pl.program_id(1) == 0)
    def _():
        s_ref[...] = jnp.zeros(s_ref.shape, F32)

    q = _ret_rope(rq_ref, cos_ref[...], sin_ref[...])
    k = _ret_rope(rk_ref, cos_ref[...], sin_ref[...])
    for h in range(RET_HEADS):
        pair = slice((h // 2) * LANES, (h // 2 + 1) * LANES)
        vcols = slice(h * RET_DV, (h + 1) * RET_DV)
        qm = _head_masked(q[:, pair], h)
        kp = k[:, pair]
        vh = rv_ref[:, vcols]
        scores = _dot_nt(qm.astype(BF16), kp.astype(BF16)) * dm_ref[h]
        state = s_ref[h]
        y_ref[:, vcols] = (_dot(scores.astype(BF16), vh)
                           + _dot((qm * vec_ref[h, _XI_F]).astype(BF16), state.astype(BF16)))
        kz = (kp * vec_ref[h, _ZETA_F]).astype(BF16)
        s_ref[h] = vec_ref[h, _CARRY_F, 0:LANES, :] * state + _dot(_transposed(kz), vh)


def _ret_bwd_body(rq_ref, rk_ref, rv_ref, rg_ref, ya_ref, cos_ref, sin_ref, vec_ref, o_ref, s_ref):
    @pl.when(pl.program_id(1) == 0)
    def _():
        s_ref[...] = jnp.zeros(s_ref.shape, F32)

    q = _ret_rope(rq_ref, cos_ref[...], sin_ref[...])
    k = _ret_rope(rk_ref, cos_ref[...], sin_ref[...])
    for h in range(RET_HEADS):
        pair = slice((h // 2) * LANES, (h // 2 + 1) * LANES)
        vcols = slice(h * RET_DV, (h + 1) * RET_DV)
        qm = _head_masked(q[:, pair], h)
        state = s_ref[h]
        y = ya_ref[:, vcols] + _dot((qm * vec_ref[h, _XI_B]).astype(BF16), state.astype(BF16))
        kz = (k[:, pair] * vec_ref[h, _ZETA_B]).astype(BF16)
        s_ref[h] = vec_ref[h, _CARRY_B, 0:LANES, :] * state + _dot(_transposed(kz), rv_ref[:, vcols])
        mu = jnp.mean(y, axis=-1, keepdims=True)
        yc = y - mu
        var = jnp.mean(yc * yc, axis=-1, keepdims=True)
        yn = yc * lax.rsqrt(var + GN_EPS)
        gate = jax.nn.silu(rg_ref[:, vcols].astype(F32))
        o_ref[:, vcols] = (gate * yn).astype(o_ref.dtype)


def _retention(rq, rk, rv, rg, cos_r, sin_r, dmask, vec, B, S, tr):
    T = rq.shape[0]
    nb = S // tr
    w = RET_HEADS * RET_DV
    fwd = lambda b, j: (b * nb + j, 0)
    bwd = lambda b, j: (b * nb + nb - 1 - j, 0)
    state = pltpu.VMEM((RET_HEADS, LANES, RET_DV), F32)
    ya = pl.pallas_call(
        _ret_fwd_body, grid=(B, nb),
        in_specs=[pl.BlockSpec((tr, rq.shape[1]), fwd), pl.BlockSpec((tr, rk.shape[1]), fwd),
                  pl.BlockSpec((tr, w), fwd), pl.BlockSpec((tr, LANES), fwd),
                  pl.BlockSpec((tr, LANES), fwd), _const_spec(dmask.shape), _const_spec(vec.shape)],
        out_specs=pl.BlockSpec((tr, w), fwd),
        out_shape=jax.ShapeDtypeStruct((T, w), F32),
        scratch_shapes=[state], compiler_params=_cparams(2), name="ret_fwd",
    )(rq, rk, rv, cos_r, sin_r, dmask, vec)
    return pl.pallas_call(
        _ret_bwd_body, grid=(B, nb),
        in_specs=[pl.BlockSpec((tr, rq.shape[1]), bwd), pl.BlockSpec((tr, rk.shape[1]), bwd),
                  pl.BlockSpec((tr, w), bwd), pl.BlockSpec((tr, w), bwd), pl.BlockSpec((tr, w), bwd),
                  pl.BlockSpec((tr, LANES), bwd), pl.BlockSpec((tr, LANES), bwd),
                  _const_spec(vec.shape)],
        out_specs=pl.BlockSpec((tr, w), bwd),
        out_shape=jax.ShapeDtypeStruct((T, w), BF16),
        scratch_shapes=[state], compiler_params=_cparams(2), name="ret_bwd",
    )(rq, rk, rv, rg, ya, cos_r, sin_r, vec)


PAIR_W = 2 * S5_GROUP_CH
PAIRS_PER_TILE = LANES // PAIR_W
CHUNK_W = S5_CHUNK * PAIR_W


def _chunk_rows(u_ref, n):
    tok = [u_ref[pl.ds(i, n, stride=S5_CHUNK), :] for i in range(S5_CHUNK)]
    return [jnp.concatenate([t[:, p * PAIR_W:(p + 1) * PAIR_W] for t in tok], axis=1)
            for p in range(PAIRS_PER_TILE)]


def _s5_in_body(u_ref, min_ref, fr_ref, fi_ref, br_ref, bi_ref):
    n = fr_ref.shape[1]
    for p, rows in enumerate(_chunk_rows(u_ref, n)):
        v = _dot_nt(rows.astype(BF16), min_ref[p])
        for part, ref in enumerate((fr_ref, fi_ref, br_ref, bi_ref)):
            ref[0, :, p * LANES:(p + 1) * LANES] = v[:, part * LANES:(part + 1) * LANES]


def _s5_scan_body(fr_ref, fi_ref, br_ref, bi_ref, a_ref, zfr_ref, zfi_ref, zbr_ref, zbi_ref, *, rb):
    n = fr_ref.shape[1]
    nblk = n // rb
    lw = fr_ref.shape[2]
    afr, afi = a_ref[0:1, :], a_ref[1:2, :]
    abr, abi = a_ref[2:3, :], a_ref[3:4, :]
    zero = jnp.zeros((1, lw), F32)

    def step(blk, carry):
        xr, xi, yr, yi = carry
        r0 = pl.multiple_of(blk * rb, rb)
        vr, vi = fr_ref[0, pl.ds(r0, rb), :], fi_ref[0, pl.ds(r0, rb), :]
        out_r, out_i = [], []
        for r in range(rb):
            out_r.append(xr)
            out_i.append(xi)
            xr, xi = (afr * xr - afi * xi + vr[r:r + 1], afr * xi + afi * xr + vi[r:r + 1])
        zfr_ref[0, pl.ds(r0, rb), :] = jnp.concatenate(out_r, axis=0).astype(zfr_ref.dtype)
        zfi_ref[0, pl.ds(r0, rb), :] = jnp.concatenate(out_i, axis=0).astype(zfi_ref.dtype)
        r1 = pl.multiple_of((nblk - 1 - blk) * rb, rb)
        wr, wi = br_ref[0, pl.ds(r1, rb), :], bi_ref[0, pl.ds(r1, rb), :]
        out_r, out_i = [None] * rb, [None] * rb
        for r in reversed(range(rb)):
            out_r[r] = yr
            out_i[r] = yi
            yr, yi = (abr * yr - abi * yi + wr[r:r + 1], abr * yi + abi * yr + wi[r:r + 1])
        zbr_ref[0, pl.ds(r1, rb), :] = jnp.concatenate(out_r, axis=0).astype(zbr_ref.dtype)
        zbi_ref[0, pl.ds(r1, rb), :] = jnp.concatenate(out_i, axis=0).astype(zbi_ref.dtype)
        return xr, xi, yr, yi

    lax.fori_loop(0, nblk, step, (zero, zero, zero, zero))


def _s5_out_body(u_ref, zfr_ref, zfi_ref, zbr_ref, zbi_ref, t_ref, mout_ref, y_ref):
    n = zfr_ref.shape[1]
    ys = []
    for p, rows in enumerate(_chunk_rows(u_ref, n)):
        cols = slice(p * LANES, (p + 1) * LANES)
        z = jnp.concatenate([r[0, :, cols] for r in (zfr_ref, zfi_ref, zbr_ref, zbi_ref)], axis=1)
        ys.append(_dot(rows.astype(BF16), t_ref[p]) + _dot(z, mout_ref[p]))
    for i in range(S5_CHUNK):
        y_ref[pl.ds(i, n, stride=S5_CHUNK), :] = jnp.concatenate(
            [y[:, i * PAIR_W:(i + 1) * PAIR_W] for y in ys], axis=1)


def _s5_core(u, m_in, t_mat, m_out, a_vec, B, S):
    N = S // S5_CHUNK
    n_tiles = S5_WIDTH // LANES
    sw = S5_GROUPS * S5_STATE
    pw = PAIRS_PER_TILE * LANES
    u_spec = pl.BlockSpec((S, LANES), lambda b, q: (b, q))
    part_spec = pl.BlockSpec((1, N, pw), lambda b, q: (b, 0, q))
    mat_spec = pl.BlockSpec((PAIRS_PER_TILE, CHUNK_W, CHUNK_W), lambda b, q: (q, 0, 0))
    parts = pl.pallas_call(
        _s5_in_body, grid=(B, n_tiles),
        in_specs=[u_spec, mat_spec], out_specs=[part_spec] * 4,
        out_shape=[jax.ShapeDtypeStruct((B, N, sw), F32)] * 4,
        compiler_params=_cparams(2), name="s5_in",
    )(u, m_in)
    lw = 512
    blk = pl.BlockSpec((1, N, lw), lambda b, j: (b, 0, j))
    z = pl.pallas_call(
        functools.partial(_s5_scan_body, rb=16), grid=(B, sw // lw),
        in_specs=[blk] * 4 + [pl.BlockSpec((4, lw), lambda b, j: (0, j))],
        out_specs=[blk] * 4, out_shape=[jax.ShapeDtypeStruct((B, N, sw), BF16)] * 4,
        compiler_params=_cparams(2), name="s5_scan",
    )(*parts, a_vec)
    return pl.pallas_call(
        _s5_out_body, grid=(B, n_tiles),
        in_specs=[u_spec] + [part_spec] * 4 + [mat_spec, mat_spec],
        out_specs=u_spec, out_shape=jax.ShapeDtypeStruct((B * S, S5_WIDTH), F32),
        compiler_params=_cparams(2), name="s5_out",
    )(u, *z, t_mat, m_out)


def _merge_body(x_ref, mod_ref, om_ref, or_ref, ys_ref, u_ref, g_ref, d_ref, wglu_ref,
                wm_ref, wr_ref, ws_ref, wo_ref, lng_ref, lnb_ref, o_ref, *, alpha):
    D = x_ref.shape[1]
    y = d_ref[...] * u_ref[...].astype(F32) + ys_ref[...].astype(F32)
    y = jax.nn.gelu(y)
    o_s5 = y * jax.nn.sigmoid(_dot(y.astype(BF16), wglu_ref[...]))
    merged = (jax.nn.sigmoid(g_ref[:, 0:D].astype(F32)) * _dot(om_ref[...], wm_ref[...])
              + jax.nn.sigmoid(g_ref[:, D:2 * D].astype(F32)) * _dot(or_ref[...], wr_ref[...])
              + jax.nn.sigmoid(g_ref[:, 2 * D:3 * D].astype(F32)) * _dot(o_s5.astype(BF16), ws_ref[...]))
    out = _dot(merged.astype(BF16), wo_ref[...])
    gate1 = mod_ref[0, :, 2 * D:3 * D]
    o_ref[...] = _layer_norm(alpha * x_ref[...] + gate1 * out) * lng_ref[...] + lnb_ref[...]


def _merge(x, mod_l, o_mla, o_ret, y_ssm, u, gates, d_skip, w_glu, wm, wr, ws, wo, ln_g, ln_b,
           tm, tiles_per_seq, alpha):
    T, D = x.shape
    row = lambda i: (i, 0)
    consts = (d_skip, w_glu, wm, wr, ws, wo, ln_g, ln_b)
    return pl.pallas_call(
        functools.partial(_merge_body, alpha=alpha), grid=(T // tm,),
        in_specs=[pl.BlockSpec((tm, D), row),
                  pl.BlockSpec((1, 1, mod_l.shape[2]), lambda i: (i // tiles_per_seq, 0, 0))]
                 + [pl.BlockSpec((tm, a.shape[1]), row) for a in (o_mla, o_ret, y_ssm, u, gates)]
                 + [_const_spec(a.shape) for a in consts],
        out_specs=pl.BlockSpec((tm, D), row),
        out_shape=jax.ShapeDtypeStruct((T, D), F32),
        compiler_params=_cparams(1), name="merge",
    )(x, mod_l, o_mla, o_ret, y_ssm, u, gates, *consts)


HALO = 8


def _ffn_body(x_ref, xp_ref, xn_ref, mod_ref, wup_ref, cw_ref, wd_ref,
              lng_ref, lnb_ref, o_ref, act_ref, *, alpha, tiles_per_seq):
    tm, D = x_ref.shape
    dff = wd_ref.shape[0]
    i = pl.program_id(0)
    shift = mod_ref[0, :, 3 * D:4 * D]
    scale = mod_ref[0, :, 4 * D:5 * D]
    gate2 = mod_ref[0, :, 5 * D:6 * D]
    x = x_ref[...]
    x_ext = jnp.concatenate([xp_ref[...], x, xn_ref[...]], axis=0)
    h = _layer_norm(x_ext) * (1.0 + scale) + shift
    r = lax.broadcasted_iota(jnp.int32, (tm + 2 * HALO, 1), 0)
    pos_in_seq = i % tiles_per_seq
    lo = jnp.where(pos_in_seq == 0, HALO, 0)
    hi = jnp.where(pos_in_seq == tiles_per_seq - 1, tm + HALO, tm + 2 * HALO)
    h = jnp.where((r >= lo) & (r < hi), h, 0.0).astype(BF16)
    n_ext = tm + 2 * HALO

    def conv_up(c0):
        cols = slice(c0, c0 + FFN_CHUNK)
        up = _dot(h, wup_ref[:, cols])
        prev = pltpu.roll(up, 1, 0)
        nxt = pltpu.roll(up, n_ext - 1, 0)
        y = (prev * cw_ref[0:1, cols] + up * cw_ref[1:2, cols] + nxt * cw_ref[2:3, cols]
             + cw_ref[3:4, cols])
        return y[HALO:HALO + tm, :]

    for c0 in range(0, dff, FFN_CHUNK):
        act_ref[:, c0:c0 + FFN_CHUNK] = (jax.nn.silu(conv_up(dff + c0)) * conv_up(c0)).astype(BF16)
    f = _dot(act_ref[...], wd_ref[...])
    o_ref[...] = _layer_norm(alpha * x + gate2 * f) * lng_ref[...] + lnb_ref[...]


def _ffn(x, mod_l, w_up, cw, wd, ln_g, ln_b, tm, tiles_per_seq, alpha):
    T, D = x.shape
    hb = tm // HALO
    n_hb = T // HALO
    consts = (w_up, cw, wd, ln_g, ln_b)
    return pl.pallas_call(
        functools.partial(_ffn_body, alpha=alpha, tiles_per_seq=tiles_per_seq), grid=(T // tm,),
        in_specs=[pl.BlockSpec((tm, D), lambda i: (i, 0)),
                  pl.BlockSpec((HALO, D), lambda i: (jnp.maximum(i * hb - 1, 0), 0)),
                  pl.BlockSpec((HALO, D), lambda i: (jnp.minimum((i + 1) * hb, n_hb - 1), 0)),
                  pl.BlockSpec((1, 1, mod_l.shape[2]), lambda i: (i // tiles_per_seq, 0, 0))]
                 + [_const_spec(a.shape) for a in consts],
        out_specs=pl.BlockSpec((tm, D), lambda i: (i, 0)),
        out_shape=jax.ShapeDtypeStruct((T, D), F32),
        scratch_shapes=[pltpu.VMEM((tm, wd.shape[0]), BF16)],
        compiler_params=_cparams(1), name="ffn",
    )(x, x, x, mod_l, *consts)


def _pack_w_in(w_in):
    widths = [MLA_Q_RANK, MLA_KV_RANK, MLA_ROPE, RET_HEADS * RET_DK, RET_HEADS * RET_DK,
              RET_HEADS * RET_DV, RET_HEADS * RET_DV, S5_WIDTH]
    sp = [0] + [int(v) for v in np.cumsum(widths)]
    w_qc, w_kvc, w_kr, w_rq, w_rk, w_rv, w_rg, w_u = [w_in[..., sp[i]:sp[i + 1]] for i in range(8)]
    w_gates = w_in[..., sp[8]:]
    w_kr = jnp.pad(w_kr, ((0, 0), (0, 0), (MLA_NOPE, HEAD_PAD - MLA_NOPE - MLA_ROPE)))
    blocks = [w_qc, w_kvc, w_kr, w_rq * (RET_DK ** -0.5), w_rk, w_rv, w_rg, w_u, w_gates]
    return jnp.concatenate(blocks, axis=-1).astype(BF16)


def _pack_mla(w_uq, w_ukv):
    L = w_uq.shape[0]
    dq = MLA_NOPE + MLA_ROPE
    wq = w_uq.reshape(L, MLA_Q_RANK, MLA_HEADS, dq)
    wq = jnp.pad(wq, ((0, 0), (0, 0), (0, 0), (0, HEAD_PAD - dq)))
    wkv = w_ukv.reshape(L, MLA_KV_RANK, MLA_HEADS, MLA_NOPE + MLA_V)
    wk = jnp.pad(wkv[..., :MLA_NOPE], ((0, 0), (0, 0), (0, 0), (0, HEAD_PAD - MLA_NOPE)))
    wv = wkv[..., MLA_NOPE:].reshape(L, MLA_KV_RANK, MLA_HEADS * MLA_V)
    flat = lambda w: w.reshape(L, w.shape[1], MLA_HEADS * HEAD_PAD).astype(BF16)
    return flat(wq), flat(wk), wv.astype(BF16)


def _ret_tables(ret_log_decay, c):
    lg = jnp.log1p(-jnp.exp(ret_log_decay.astype(F32)))
    lg_f, lg_b = lg[:, 0, :, None, None], lg[:, 1, :, None, None]
    idx = jnp.arange(c, dtype=F32)
    diff = idx[:, None] - idx[None, :]
    dmask = jnp.where(diff >= 0, jnp.exp(lg_f * jnp.maximum(diff, 0.0)),
                      jnp.exp(lg_b * jnp.maximum(-diff, 0.0)))
    lf, lb = lg[:, 0, :, None], lg[:, 1, :, None]
    ones = jnp.ones((c,), F32)
    cols = [jnp.exp(lf * (idx + 1)), jnp.exp(lf * (c - 1 - idx)), jnp.exp(lb * (c - idx)),
            jnp.exp(lb * idx), jnp.exp(lf * c) * ones, jnp.exp(lb * c) * ones]
    vec = jnp.stack(cols, axis=2)
    return dmask, jnp.broadcast_to(vec[..., None], vec.shape + (LANES,))


def _s5_tables(lam_re, lam_im, log_step, b_re, b_im, c_re, c_im):
    L = lam_re.shape[0]
    G, P, CH, C = S5_GROUPS, S5_STATE, S5_GROUP_CH, S5_CHUNK
    f32 = lambda t: t.astype(F32)
    lam_re, lam_im, b_re, b_im, c_re, c_im = map(f32, (lam_re, lam_im, b_re, b_im, c_re, c_im))
    step = jnp.exp(f32(log_step))[..., None]
    mag = jnp.exp(lam_re * step)
    a_re, a_im = mag * jnp.cos(lam_im * step), mag * jnp.sin(lam_im * step)
    den = jnp.square(lam_re) + jnp.square(lam_im)
    f_re = ((a_re - 1.0) * lam_re + a_im * lam_im) / den
    f_im = (a_im * lam_re - (a_re - 1.0) * lam_im) / den
    bb_re = f_re[..., None] * b_re - f_im[..., None] * b_im
    bb_im = f_re[..., None] * b_im + f_im[..., None] * b_re
    k = jnp.arange(C + 1, dtype=F32)[:, None, None, None, None]
    pm = jnp.exp(k * (lam_re * step))
    pw_re, pw_im = pm * jnp.cos(k * (lam_im * step)), pm * jnp.sin(k * (lam_im * step))

    NP = S5_PAIRS
    lane = np.arange(CHUNK_W)
    tok, grp, ch = lane // PAIR_W, (lane // CH) % 2, lane % CH
    onehot = lambda idx, n: jnp.asarray(np.arange(n)[:, None] == idx[None, :], F32)
    spread = lambda t, sel: jnp.einsum('...k,kn->...n', t, sel, precision=HI)
    ch_sel = onehot(ch, CH)
    own_lanes = jnp.asarray(grp[None, :] == (np.arange(G) % 2)[:, None], F32)[:, None, :]

    def times(z, mat_re, mat_im, expo):
        sel = onehot(expo, C + 1)
        pr = spread(jnp.moveaxis(pw_re[:, :, z], 0, -1), sel) * own_lanes
        pi = spread(jnp.moveaxis(pw_im[:, :, z], 0, -1), sel) * own_lanes
        mr, mi = spread(mat_re[:, z], ch_sel), spread(mat_im[:, z], ch_sel)
        return mr * pr - mi * pi, mr * pi + mi * pr

    cr, ci = jnp.swapaxes(c_re, -1, -2), jnp.swapaxes(c_im, -1, -2)
    pairs = lambda t: t.reshape((L, NP, 2) + t.shape[2:])

    of_re, of_im = times(0, cr, ci, tok + 1)
    ob_re, ob_im = times(1, cr, ci, C - tok)
    m_out = jnp.stack([pairs(t) for t in (of_re, -of_im, ob_re, -ob_im)], axis=2)
    m_out = m_out.reshape(L * NP, 4 * 2 * P, CHUNK_W)
    if_re, if_im = times(0, bb_re, bb_im, C - 1 - tok)
    ib_re, ib_im = times(1, bb_re, bb_im, tok)
    m_in_t = jnp.stack([pairs(t) for t in (if_re, if_im, ib_re, ib_im)], axis=2)
    m_in_t = m_in_t.reshape(L * NP, 4 * 2 * P, CHUNK_W)

    def lag_row(z, expo):
        h_re, h_im = times(z, cr, ci, expo)
        bt_re, bt_im = jnp.swapaxes(bb_re[:, z], -1, -2), jnp.swapaxes(bb_im[:, z], -1, -2)
        return (jnp.einsum('lgcp,lgpn->lgcn', bt_re, h_re, precision=HI)
                - jnp.einsum('lgcp,lgpn->lgcn', bt_im, h_im, precision=HI))

    row_f, row_b = lag_row(0, tok), lag_row(1, C - 1 - tok)
    lane_i = jnp.asarray(lane)
    blocks = []
    for j in range(C):
        fwd = jnp.where(lane_i >= j * PAIR_W, jnp.roll(row_f, j * PAIR_W, axis=-1), 0.0)
        bwd = jnp.where(lane_i < (j + 1) * PAIR_W, jnp.roll(row_b, -(C - 1 - j) * PAIR_W, axis=-1), 0.0)
        blocks.append(pairs(fwd + bwd))
    t_p = jnp.stack(blocks, axis=2).reshape(L * NP, CHUNK_W, CHUNK_W)

    a_vec = jnp.stack([pw_re[C, :, 0], pw_im[C, :, 0], pw_re[C, :, 1], pw_im[C, :, 1]], axis=1)
    a_vec = a_vec.reshape(L, 4, G * P)
    return t_p.astype(BF16), m_in_t.astype(BF16), m_out.astype(BF16), a_vec


def _pack_ffn(w_up, conv_w, conv_b, w_down):
    cw = jnp.concatenate([conv_w, conv_b[:, None, :]], axis=1)
    return w_up.astype(BF16), cw, w_down.astype(BF16)


def _tile(n, pref):
    t = min(pref, n)
    assert n % t == 0, (n, t)
    return t


def kernel(x, c, positions, w_in, mla_q_norm, mla_w_uq, mla_kv_norm, mla_w_ukv, ret_log_decay, s5_lam_re, s5_lam_im, s5_log_step, s5_b_re, s5_b_im, s5_c_re, s5_c_im, s5_d, s5_w_glu, w_branch_mla, w_branch_ret, w_branch_s5, w_o, ffn_w_up, ffn_conv_w, ffn_conv_b, ffn_w_down, ln1_g, ln1_b, ln2_g, ln2_b, w_ada, b_ada):
    B, S, D = x.shape
    L = w_in.shape[0]
    T = B * S
    assert S % LANES == 0 and S % (16 * S5_CHUNK) == 0 and ffn_w_up.shape[2] % (2 * FFN_CHUNK) == 0
    alpha = (2 * L) ** 0.25
    tm = _tile(S, 512)
    tps = S // tm

    cos_m, sin_m, cos_r, sin_r = _rope_tables(positions, _tile(T, 2048))
    mod = _ada_mod(c, w_ada, b_ada).reshape(L, B, 1, -1)

    w_packed = _pack_w_in(w_in)
    wq, wk, wv = _pack_mla(mla_w_uq, mla_w_ukv)
    dmask, ret_vec = _ret_tables(ret_log_decay, tm)
    t_mat, m_in, m_out, a_vec = _s5_tables(s5_lam_re, s5_lam_im, s5_log_step, s5_b_re, s5_b_im,
                                           s5_c_re, s5_c_im)
    w_up, conv_cw, w_down = _pack_ffn(ffn_w_up, ffn_conv_w, ffn_conv_b, ffn_w_down)
    bf = lambda w: w.astype(BF16)
    row2 = lambda v: v.reshape(L, 1, -1)
    q_gain, kv_gain = row2(mla_q_norm), row2(mla_kv_norm)
    d_skip = row2(s5_d)
    g1, b1, g2, b2 = row2(ln1_g), row2(ln1_b), row2(ln2_g), row2(ln2_b)
    w_glu, wbm, wbr, wbs, wo = map(bf, (s5_w_glu, w_branch_mla, w_branch_ret, w_branch_s5, w_o))
    NP = S5_PAIRS

    xf = x.reshape(T, D)
    for l in range(L):
        q, k, v, rq, rk, rv, rg, u, gates = _in_proj(xf, mod[l], w_packed[l], cos_m, sin_m, q_gain[l],
                                                     kv_gain[l], wq[l], wk[l], wv[l], tm, tps)
        o_mla = _mla_attn(q, k, v, B, S, _tile(S, 1024), _tile(S, 512))
        o_ret = _retention(rq, rk, rv, rg, cos_r, sin_r, dmask[l], ret_vec[l], B, S, tm)
        y_ssm = _s5_core(u, m_in[l * NP:(l + 1) * NP], t_mat[l * NP:(l + 1) * NP],
                         m_out[l * NP:(l + 1) * NP], a_vec[l], B, S)
        x1 = _merge(xf, mod[l], o_mla, o_ret, y_ssm, u, gates, d_skip[l],
                    w_glu[l], wbm[l], wbr[l], wbs[l], wo[l], g1[l], b1[l], tm, tps, alpha)
        xf = _ffn(x1, mod[l], w_up[l], conv_cw[l], w_down[l], g2[l], b2[l], tm, tps, alpha)
    return xf.reshape(B, S, D)
```

```python
import functools
import math

import jax
import jax.numpy as jnp
import numpy as np
from jax import lax
from jax.experimental import pallas as pl
from jax.experimental.pallas import tpu as pltpu

F32 = jnp.float32
BF16 = jnp.bfloat16
HI = lax.Precision.HIGHEST

MLA_HEADS = 8
MLA_Q_RANK = 256
MLA_KV_RANK = 128
MLA_NOPE = 64
MLA_ROPE = 32
MLA_V = 64
RET_HEADS = 4
RET_DK = 64
RET_DV = 128
S5_GROUP_CH = 16
S5_WIDTH = 512
S5_GROUPS = S5_WIDTH // S5_GROUP_CH
S5_STATE = 64
S5_CHUNK = 16
S5_PAIRS = S5_GROUPS // 2
ROPE_BASE = 10000.0
LN_EPS = 1e-5
RMS_EPS = 1e-6
GN_EPS = 1e-5
FFN_CHUNK = 256

LANES = 128
HEAD_PAD = 128
VMEM_LIMIT = 56 * 1024 * 1024


def _cparams(n_axes):
    return pltpu.CompilerParams(dimension_semantics=("arbitrary",) * n_axes,
                                vmem_limit_bytes=VMEM_LIMIT)


def _const_spec(shape):
    nd = len(shape)
    return pl.BlockSpec(shape, lambda *_: (0,) * nd)


def _layer_norm(x):
    mu = jnp.mean(x, axis=-1, keepdims=True)
    xc = x - mu
    var = jnp.mean(xc * xc, axis=-1, keepdims=True)
    return xc * lax.rsqrt(var + LN_EPS)


def _rms_norm(x, g):
    return x * lax.rsqrt(jnp.mean(x * x, axis=-1, keepdims=True) + RMS_EPS) * g


def _dot(a, b):
    return jnp.dot(a, b, preferred_element_type=F32)


def _dot_nt(a, b):
    return lax.dot_general(a, b, (((1,), (1,)), ((), ())), preferred_element_type=F32)


def _transposed(a):
    n = a.shape[1]
    eye = (lax.broadcasted_iota(jnp.int32, (n, n), 0)
           == lax.broadcasted_iota(jnp.int32, (n, n), 1)).astype(a.dtype)
    return _dot_nt(eye, a).astype(a.dtype)


def _rope_body(pos_ref, invf_ref, cm_ref, sm_ref, cr_ref, sr_ref):
    p = pos_ref[...].astype(F32)
    ang_m = p * invf_ref[0:1, :]
    ang_r = p * invf_ref[1:2, :]
    cm_ref[...] = jnp.cos(ang_m)
    sm_ref[...] = jnp.sin(ang_m)
    cr_ref[...] = jnp.cos(ang_r)
    sr_ref[...] = jnp.sin(ang_r)


def _rope_tables(positions, tm):
    T = positions.size
    half_m, half_r = MLA_ROPE // 2, RET_DK // 2
    inv_m = ROPE_BASE ** (-jnp.arange(half_m, dtype=F32) / half_m)
    inv_r = ROPE_BASE ** (-jnp.arange(half_r, dtype=F32) / half_r)
    lane = np.arange(LANES)
    row_m = jnp.where((lane >= MLA_NOPE) & (lane < MLA_NOPE + MLA_ROPE),
                      inv_m[(lane - MLA_NOPE) % half_m], 0.0)
    row_r = inv_r[lane % half_r]
    invf = jnp.stack([row_m, row_r]).astype(F32)
    out = jax.ShapeDtypeStruct((T, LANES), F32)
    spec = pl.BlockSpec((tm, LANES), lambda i: (i, 0))
    return pl.pallas_call(
        _rope_body, grid=(T // tm,),
        in_specs=[pl.BlockSpec((tm, 1), lambda i: (i, 0)), _const_spec((2, LANES))],
        out_specs=[spec] * 4, out_shape=[out] * 4,
        compiler_params=_cparams(1), name="rope_tables",
    )(positions.reshape(T, 1), invf)


def _mod_body(c_ref, w_ref, b_ref, o_ref):
    cond = jax.nn.silu(c_ref[...]).astype(BF16)
    o_ref[0] = _dot(cond, w_ref[0].astype(BF16)) + b_ref[0]


def _ada_mod(c, w_ada, b_ada):
    L, D, N = w_ada.shape
    B = c.shape[0]
    tn = D
    return pl.pallas_call(
        _mod_body, grid=(L, N // tn),
        in_specs=[_const_spec((B, D)),
                  pl.BlockSpec((1, D, tn), lambda l, j: (l, 0, j)),
                  pl.BlockSpec((1, 1, tn), lambda l, j: (l, 0, j))],
        out_specs=pl.BlockSpec((1, B, tn), lambda l, j: (l, 0, j)),
        out_shape=jax.ShapeDtypeStruct((L, B, N), F32),
        compiler_params=_cparams(2), name="ada_mod",
    )(c, w_ada, b_ada.reshape(L, 1, N))


_MLA_IN = MLA_Q_RANK + MLA_KV_RANK + HEAD_PAD
_IN_BLOCKS = (("rq", 256, BF16), ("rk", 256, BF16), ("rv", 512, BF16), ("rg", 512, BF16),
              ("u", 512, F32), ("gates", 3072, BF16))


def _in_proj_body(x_ref, mod_ref, w_ref, cos_ref, sin_ref, qg_ref, kvg_ref, wq_ref, wk_ref, wv_ref,
                  q_ref, k_ref, v_ref, *out_refs):
    D = x_ref.shape[1]
    xn = _layer_norm(x_ref[...])
    shift = mod_ref[0, :, 0:D]
    scale = mod_ref[0, :, D:2 * D]
    h = (xn * (1.0 + scale) + shift).astype(BF16)

    lat = _dot(h, w_ref[:, 0:_MLA_IN])
    cos, sin = cos_ref[...], sin_ref[...]
    qn = _rms_norm(lat[:, :MLA_Q_RANK], qg_ref[...]).astype(BF16)
    q_scale = (MLA_NOPE + MLA_ROPE) ** -0.5 * math.log2(math.e)
    q_ref[...] = (_mla_rope(_dot(qn, wq_ref[...]), cos, sin) * q_scale).astype(BF16)
    kvn = _rms_norm(lat[:, MLA_Q_RANK:MLA_Q_RANK + MLA_KV_RANK], kvg_ref[...]).astype(BF16)
    k_rope = _mla_rope(lat[:, MLA_Q_RANK + MLA_KV_RANK:], cos, sin)
    k = _dot(kvn, wk_ref[...]) + jnp.concatenate([k_rope] * MLA_HEADS, axis=1)
    k_ref[...] = k.astype(BF16)
    v_ref[...] = _dot(kvn, wv_ref[...]).astype(BF16)

    lo = _MLA_IN
    for (_, width, _), o_ref in zip(_IN_BLOCKS, out_refs):
        for c0 in range(0, width, 512):
            c1 = min(c0 + 512, width)
            o_ref[:, c0:c1] = _dot(h, w_ref[:, lo + c0:lo + c1]).astype(o_ref.dtype)
        lo += width


def _in_proj(x, mod_l, w_packed, cos_m, sin_m, q_gain, kv_gain, wq, wk, wv, tm, tiles_per_seq):
    T, D = x.shape
    row = lambda i: (i, 0)
    consts = (q_gain, kv_gain, wq, wk, wv)
    mla_out = [(MLA_HEADS * HEAD_PAD, BF16), (MLA_HEADS * HEAD_PAD, BF16), (MLA_HEADS * MLA_V, BF16)]
    outs = mla_out + [(w, dt) for _, w, dt in _IN_BLOCKS]
    return pl.pallas_call(
        _in_proj_body, grid=(T // tm,),
        in_specs=[pl.BlockSpec((tm, D), row),
                  pl.BlockSpec((1, 1, mod_l.shape[2]), lambda i: (i // tiles_per_seq, 0, 0)),
                  _const_spec(w_packed.shape),
                  pl.BlockSpec((tm, LANES), row), pl.BlockSpec((tm, LANES), row)]
                 + [_const_spec(a.shape) for a in consts],
        out_specs=[pl.BlockSpec((tm, w), row) for w, _ in outs],
        out_shape=[jax.ShapeDtypeStruct((T, w), dt) for w, dt in outs],
        compiler_params=_cparams(1), name="in_proj",
    )(x, mod_l, w_packed, cos_m, sin_m, *consts)


def _rot_half(x, lo, half):
    lane = lax.broadcasted_iota(jnp.int32, x.shape, 1)
    up = pltpu.roll(x, LANES - half, 1)
    down = pltpu.roll(x, half, 1)
    if lo is None:
        return jnp.where(lane % (2 * half) < half, -up, down)
    return jnp.where((lane >= lo) & (lane < lo + half), -up,
                     jnp.where((lane >= lo + half) & (lane < lo + 2 * half), down, 0.0))


def _mla_rope(x, cos, sin):
    tiles = [x[:, i * HEAD_PAD:(i + 1) * HEAD_PAD] for i in range(x.shape[1] // HEAD_PAD)]
    return jnp.concatenate([t * cos + _rot_half(t, MLA_NOPE, MLA_ROPE // 2) * sin for t in tiles],
                           axis=1)


def _attn_body(q_ref, k_ref, v_ref, o_ref, m_ref, l_ref, acc_ref, *, tk):
    tq = q_ref.shape[0]
    nk = k_ref.shape[0] // tk
    nt = tk // LANES
    heads = [slice(j * HEAD_PAD, (j + 1) * HEAD_PAD) for j in range(2)]
    m_ref[...] = jnp.full(m_ref.shape, -jnp.inf, F32)
    l_ref[...] = jnp.zeros(l_ref.shape, F32)
    acc_ref[...] = jnp.zeros(acc_ref.shape, F32)

    def kstep(t, carry):
        r0 = pl.multiple_of(t * tk, tk)
        v = v_ref[pl.ds(r0, tk), :]
        s = [_dot_nt(q_ref[:, heads[j]], k_ref[pl.ds(r0, tk), heads[j]]) for j in range(2)]
        for j in range(2):
            tiles = [s[j][:, i * LANES:(i + 1) * LANES] for i in range(nt)]
            m_prev = m_ref[j]
            m_new = jnp.maximum(m_prev, jnp.max(functools.reduce(jnp.maximum, tiles),
                                                axis=1, keepdims=True))
            alpha = jnp.exp2(m_prev - m_new)
            p = [jnp.exp2(t_ - m_new) for t_ in tiles]
            l_ref[j] = alpha * l_ref[j] + functools.reduce(jnp.add, p)
            m_ref[j] = m_new
            pb = jnp.concatenate([t_.astype(BF16) for t_ in p], axis=1)
            acc_ref[j] = alpha * acc_ref[j] + _dot(pb, v)
        return carry

    lax.fori_loop(0, nk, kstep, 0, unroll=4)
    outs = [acc_ref[j] / jnp.sum(l_ref[j], axis=1, keepdims=True) for j in range(2)]
    lane = lax.broadcasted_iota(jnp.int32, (tq, 2 * MLA_V), 1)
    o_ref[...] = jnp.where(lane < MLA_V, outs[0], outs[1]).astype(o_ref.dtype)


def _mla_attn(q, k, v, B, S, tq, tk):
    T = q.shape[0]
    nq = S // tq
    pairs = MLA_HEADS // 2
    return pl.pallas_call(
        functools.partial(_attn_body, tk=tk), grid=(B, pairs, nq),
        in_specs=[pl.BlockSpec((tq, 2 * HEAD_PAD), lambda b, h, i: (b * nq + i, h)),
                  pl.BlockSpec((S, 2 * HEAD_PAD), lambda b, h, i: (b, h)),
                  pl.BlockSpec((S, 2 * MLA_V), lambda b, h, i: (b, h))],
        out_specs=pl.BlockSpec((tq, 2 * MLA_V), lambda b, h, i: (b * nq + i, h)),
        out_shape=jax.ShapeDtypeStruct((T, MLA_HEADS * MLA_V), BF16),
        scratch_shapes=[pltpu.VMEM((2, tq, LANES), F32), pltpu.VMEM((2, tq, LANES), F32),
                        pltpu.VMEM((2, tq, 2 * MLA_V), F32)],
        compiler_params=_cparams(3), name="mla_attn",
    )(q, k, v)


_XI_F, _ZETA_F, _XI_B, _ZETA_B, _CARRY_F, _CARRY_B = range(6)


def _ret_rope(r_ref, cos, sin):
    r = r_ref[...].astype(F32)
    tiles = [r[:, i * LANES:(i + 1) * LANES] for i in range(r.shape[1] // LANES)]
    return jnp.concatenate([t * cos + _rot_half(t, None, RET_DK // 2) * sin for t in tiles], axis=1)


def _head_masked(x_pair, h):
    lane = lax.broadcasted_iota(jnp.int32, x_pair.shape, 1)
    lo = (h % 2) * RET_DK
    return jnp.where((lane >= lo) & (lane < lo + RET_DK), x_pair, 0.0)


def _ret_fwd_body(rq_ref, rk_ref, rv_ref, cos_ref, sin_ref, dm_ref, vec_ref, y_ref, s_ref):
    @pl.when(pl.program_id(1) == 0)
    def _():
        s_ref[...] = jnp.zeros(s_ref.shape, F32)

    q = _ret_rope(rq_ref, cos_ref[...], sin_ref[...])
    k = _ret_rope(rk_ref, cos_ref[...], sin_ref[...])
    for h in range(RET_HEADS):
        pair = slice((h // 2) * LANES, (h // 2 + 1) * LANES)
        vcols = slice(h * RET_DV, (h + 1) * RET_DV)
        qm = _head_masked(q[:, pair], h)
        kp = k[:, pair]
        vh = rv_ref[:, vcols]
        scores = _dot_nt(qm.astype(BF16), kp.astype(BF16)) * dm_ref[h]
        state = s_ref[h]
        y_ref[:, vcols] = (_dot(scores.astype(BF16), vh)
                           + _dot((qm * vec_ref[h, _XI_F]).astype(BF16), state.astype(BF16)))
        kz = (kp * vec_ref[h, _ZETA_F]).astype(BF16)
        s_ref[h] = vec_ref[h, _CARRY_F, 0:LANES, :] * state + _dot(_transposed(kz), vh)


def _ret_bwd_body(rq_ref, rk_ref, rv_ref, rg_ref, ya_ref, cos_ref, sin_ref, vec_ref, o_ref, s_ref):
    @pl.when(pl.program_id(1) == 0)
    def _():
        s_ref[...] = jnp.zeros(s_ref.shape, F32)

    q = _ret_rope(rq_ref, cos_ref[...], sin_ref[...])
    k = _ret_rope(rk_ref, cos_ref[...], sin_ref[...])
    for h in range(RET_HEADS):
        pair = slice((h // 2) * LANES, (h // 2 + 1) * LANES)
        vcols = slice(h * RET_DV, (h + 1) * RET_DV)
        qm = _head_masked(q[:, pair], h)
        state = s_ref[h]
        y = ya_ref[:, vcols] + _dot((qm * vec_ref[h, _XI_B]).astype(BF16), state.astype(BF16))
        kz = (k[:, pair] * vec_ref[h, _ZETA_B]).astype(BF16)
        s_ref[h] = vec_ref[h, _CARRY_B, 0:LANES, :] * state + _dot(_transposed(kz), rv_ref[:, vcols])
        mu = jnp.mean(y, axis=-1, keepdims=True)
        yc = y - mu
        var = jnp.mean(yc * yc, axis=-1, keepdims=True)
        yn = yc * lax.rsqrt(var + GN_EPS)
        gate = jax.nn.silu(rg_ref[:, vcols].astype(F32))
        o_ref[:, vcols] = (gate * yn).astype(o_ref.dtype)


def _retention(rq, rk, rv, rg, cos_r, sin_r, dmask, vec, B, S, tr):
    T = rq.shape[0]
    nb = S // tr
    w = RET_HEADS * RET_DV
    fwd = lambda b, j: (b * nb + j, 0)
    bwd = lambda b, j: (b * nb + nb - 1 - j, 0)
    state = pltpu.VMEM((RET_HEADS, LANES, RET_DV), F32)
    ya = pl.pallas_call(
        _ret_fwd_body, grid=(B, nb),
        in_specs=[pl.BlockSpec((tr, rq.shape[1]), fwd), pl.BlockSpec((tr, rk.shape[1]), fwd),
                  pl.BlockSpec((tr, w), fwd), pl.BlockSpec((tr, LANES), fwd),
                  pl.BlockSpec((tr, LANES), fwd), _const_spec(dmask.shape), _const_spec(vec.shape)],
        out_specs=pl.BlockSpec((tr, w), fwd),
        out_shape=jax.ShapeDtypeStruct((T, w), F32),
        scratch_shapes=[state], compiler_params=_cparams(2), name="ret_fwd",
    )(rq, rk, rv, cos_r, sin_r, dmask, vec)
    return pl.pallas_call(
        _ret_bwd_body, grid=(B, nb),
        in_specs=[pl.BlockSpec((tr, rq.shape[1]), bwd), pl.BlockSpec((tr, rk.shape[1]), bwd),
                  pl.BlockSpec((tr, w), bwd), pl.BlockSpec((tr, w), bwd), pl.BlockSpec((tr, w), bwd),
                  pl.BlockSpec((tr, LANES), bwd), pl.BlockSpec((tr, LANES), bwd),
                  _const_spec(vec.shape)],
        out_specs=pl.BlockSpec((tr, w), bwd),
        out_shape=jax.ShapeDtypeStruct((T, w), BF16),
        scratch_shapes=[state], compiler_params=_cparams(2), name="ret_bwd",
    )(rq, rk, rv, rg, ya, cos_r, sin_r, vec)


PAIR_W = 2 * S5_GROUP_CH
PAIRS_PER_TILE = LANES // PAIR_W
CHUNK_W = S5_CHUNK * PAIR_W


def _chunk_rows(u_ref, n):
    tok = [u_ref[pl.ds(i, n, stride=S5_CHUNK), :] for i in range(S5_CHUNK)]
    return [jnp.concatenate([t[:, p * PAIR_W:(p + 1) * PAIR_W] for t in tok], axis=1)
            for p in range(PAIRS_PER_TILE)]


def _s5_in_body(u_ref, min_ref, fr_ref, fi_ref, br_ref, bi_ref):
    n = fr_ref.shape[1]
    for p, rows in enumerate(_chunk_rows(u_ref, n)):
        v = _dot_nt(rows.astype(BF16), min_ref[p])
        for part, ref in enumerate((fr_ref, fi_ref, br_ref, bi_ref)):
            ref[0, :, p * LANES:(p + 1) * LANES] = v[:, part * LANES:(part + 1) * LANES]


def _s5_scan_body(fr_ref, fi_ref, br_ref, bi_ref, a_ref, zfr_ref, zfi_ref, zbr_ref, zbi_ref, *, rb):
    n = fr_ref.shape[1]
    nblk = n // rb
    lw = fr_ref.shape[2]
    afr, afi = a_ref[0:1, :], a_ref[1:2, :]
    abr, abi = a_ref[2:3, :], a_ref[3:4, :]
    zero = jnp.zeros((1, lw), F32)

    def step(blk, carry):
        xr, xi, yr, yi = carry
        r0 = pl.multiple_of(blk * rb, rb)
        vr, vi = fr_ref[0, pl.ds(r0, rb), :], fi_ref[0, pl.ds(r0, rb), :]
        out_r, out_i = [], []
        for r in range(rb):
            out_r.append(xr)
            out_i.append(xi)
            xr, xi = (afr * xr - afi * xi + vr[r:r + 1], afr * xi + afi * xr + vi[r:r + 1])
        zfr_ref[0, pl.ds(r0, rb), :] = jnp.concatenate(out_r, axis=0).astype(zfr_ref.dtype)
        zfi_ref[0, pl.ds(r0, rb), :] = jnp.concatenate(out_i, axis=0).astype(zfi_ref.dtype)
        r1 = pl.multiple_of((nblk - 1 - blk) * rb, rb)
        wr, wi = br_ref[0, pl.ds(r1, rb), :], bi_ref[0, pl.ds(r1, rb), :]
        out_r, out_i = [None] * rb, [None] * rb
        for r in reversed(range(rb)):
            out_r[r] = yr
            out_i[r] = yi
            yr, yi = (abr * yr - abi * yi + wr[r:r + 1], abr * yi + abi * yr + wi[r:r + 1])
        zbr_ref[0, pl.ds(r1, rb), :] = jnp.concatenate(out_r, axis=0).astype(zbr_ref.dtype)
        zbi_ref[0, pl.ds(r1, rb), :] = jnp.concatenate(out_i, axis=0).astype(zbi_ref.dtype)
        return xr, xi, yr, yi

    lax.fori_loop(0, nblk, step, (zero, zero, zero, zero))


def _s5_out_body(u_ref, zfr_ref, zfi_ref, zbr_ref, zbi_ref, t_ref, mout_ref, y_ref):
    n = zfr_ref.shape[1]
    ys = []
    for p, rows in enumerate(_chunk_rows(u_ref, n)):
        cols = slice(p * LANES, (p + 1) * LANES)
        z = jnp.concatenate([r[0, :, cols] for r in (zfr_ref, zfi_ref, zbr_ref, zbi_ref)], axis=1)
        ys.append(_dot(rows.astype(BF16), t_ref[p]) + _dot(z, mout_ref[p]))
    for i in range(S5_CHUNK):
        y_ref[pl.ds(i, n, stride=S5_CHUNK), :] = jnp.concatenate(
            [y[:, i * PAIR_W:(i + 1) * PAIR_W] for y in ys], axis=1)


def _s5_core(u, m_in, t_mat, m_out, a_vec, B, S):
    N = S // S5_CHUNK
    n_tiles = S5_WIDTH // LANES
    sw = S5_GROUPS * S5_STATE
    pw = PAIRS_PER_TILE * LANES
    u_spec = pl.BlockSpec((S, LANES), lambda b, q: (b, q))
    part_spec = pl.BlockSpec((1, N, pw), lambda b, q: (b, 0, q))
    mat_spec = pl.BlockSpec((PAIRS_PER_TILE, CHUNK_W, CHUNK_W), lambda b, q: (q, 0, 0))
    parts = pl.pallas_call(
        _s5_in_body, grid=(B, n_tiles),
        in_specs=[u_spec, mat_spec], out_specs=[part_spec] * 4,
        out_shape=[jax.ShapeDtypeStruct((B, N, sw), F32)] * 4,
        compiler_params=_cparams(2), name="s5_in",
    )(u, m_in)
    lw = 512
    blk = pl.BlockSpec((1, N, lw), lambda b, j: (b, 0, j))
    z = pl.pallas_call(
        functools.partial(_s5_scan_body, rb=16), grid=(B, sw // lw),
        in_specs=[blk] * 4 + [pl.BlockSpec((4, lw), lambda b, j: (0, j))],
        out_specs=[blk] * 4, out_shape=[jax.ShapeDtypeStruct((B, N, sw), BF16)] * 4,
        compiler_params=_cparams(2), name="s5_scan",
    )(*parts, a_vec)
    return pl.pallas_call(
        _s5_out_body, grid=(B, n_tiles),
        in_specs=[u_spec] + [part_spec] * 4 + [mat_spec, mat_spec],
        out_specs=u_spec, out_shape=jax.ShapeDtypeStruct((B * S, S5_WIDTH), F32),
        compiler_params=_cparams(2), name="s5_out",
    )(u, *z, t_mat, m_out)


def _merge_body(x_ref, mod_ref, om_ref, or_ref, ys_ref, u_ref, g_ref, d_ref, wglu_ref,
                wm_ref, wr_ref, ws_ref, wo_ref, lng_ref, lnb_ref, o_ref, *, alpha):
    D = x_ref.shape[1]
    y = d_ref[...] * u_ref[...].astype(F32) + ys_ref[...].astype(F32)
    y = jax.nn.gelu(y)
    o_s5 = y * jax.nn.sigmoid(_dot(y.astype(BF16), wglu_ref[...]))
    merged = (jax.nn.sigmoid(g_ref[:, 0:D].astype(F32)) * _dot(om_ref[...], wm_ref[...])
              + jax.nn.sigmoid(g_ref[:, D:2 * D].astype(F32)) * _dot(or_ref[...], wr_ref[...])
              + jax.nn.sigmoid(g_ref[:, 2 * D:3 * D].astype(F32)) * _dot(o_s5.astype(BF16), ws_ref[...]))
    out = _dot(merged.astype(BF16), wo_ref[...])
    gate1 = mod_ref[0, :, 2 * D:3 * D]
    o_ref[...] = _layer_norm(alpha * x_ref[...] + gate1 * out) * lng_ref[...] + lnb_ref[...]


def _merge(x, mod_l, o_mla, o_ret, y_ssm, u, gates, d_skip, w_glu, wm, wr, ws, wo, ln_g, ln_b,
           tm, tiles_per_seq, alpha):
    T, D = x.shape
    row = lambda i: (i, 0)
    consts = (d_skip, w_glu, wm, wr, ws, wo, ln_g, ln_b)
    return pl.pallas_call(
        functools.partial(_merge_body, alpha=alpha), grid=(T // tm,),
        in_specs=[pl.BlockSpec((tm, D), row),
                  pl.BlockSpec((1, 1, mod_l.shape[2]), lambda i: (i // tiles_per_seq, 0, 0))]
                 + [pl.BlockSpec((tm, a.shape[1]), row) for a in (o_mla, o_ret, y_ssm, u, gates)]
                 + [_const_spec(a.shape) for a in consts],
        out_specs=pl.BlockSpec((tm, D), row),
        out_shape=jax.ShapeDtypeStruct((T, D), F32),
        compiler_params=_cparams(1), name="merge",
    )(x, mod_l, o_mla, o_ret, y_ssm, u, gates, *consts)


HALO = 8


def _ffn_body(x_ref, xp_ref, xn_ref, mod_ref, wup_ref, cw_ref, wd_ref,
              lng_ref, lnb_ref, o_ref, act_ref, *, alpha, tiles_per_seq):
    tm, D = x_ref.shape
    dff = wd_ref.shape[0]
    i = pl.program_id(0)
    shift = mod_ref[0, :, 3 * D:4 * D]
    scale = mod_ref[0, :, 4 * D:5 * D]
    gate2 = mod_ref[0, :, 5 * D:6 * D]
    x = x_ref[...]
    x_ext = jnp.concatenate([xp_ref[...], x, xn_ref[...]], axis=0)
    h = _layer_norm(x_ext) * (1.0 + scale) + shift
    r = lax.broadcasted_iota(jnp.int32, (tm + 2 * HALO, 1), 0)
    pos_in_seq = i % tiles_per_seq
    lo = jnp.where(pos_in_seq == 0, HALO, 0)
    hi = jnp.where(pos_in_seq == tiles_per_seq - 1, tm + HALO, tm + 2 * HALO)
    h = jnp.where((r >= lo) & (r < hi), h, 0.0).astype(BF16)
    n_ext = tm + 2 * HALO

    def conv_up(c0):
        cols = slice(c0, c0 + FFN_CHUNK)
        up = _dot(h, wup_ref[:, cols])
        prev = pltpu.roll(up, 1, 0)
        nxt = pltpu.roll(up, n_ext - 1, 0)
        y = (prev * cw_ref[0:1, cols] + up * cw_ref[1:2, cols] + nxt * cw_ref[2:3, cols]
             + cw_ref[3:4, cols])
        return y[HALO:HALO + tm, :]

    for c0 in range(0, dff, FFN_CHUNK):
        act_ref[:, c0:c0 + FFN_CHUNK] = (jax.nn.silu(conv_up(dff + c0)) * conv_up(c0)).astype(BF16)
    f = _dot(act_ref[...], wd_ref[...])
    o_ref[...] = _layer_norm(alpha * x + gate2 * f) * lng_ref[...] + lnb_ref[...]


def _ffn(x, mod_l, w_up, cw, wd, ln_g, ln_b, tm, tiles_per_seq, alpha):
    T, D = x.shape
    hb = tm // HALO
    n_hb = T // HALO
    consts = (w_up, cw, wd, ln_g, ln_b)
    return pl.pallas_call(
        functools.partial(_ffn_body, alpha=alpha, tiles_per_seq=tiles_per_seq), grid=(T // tm,),
        in_specs=[pl.BlockSpec((tm, D), lambda i: (i, 0)),
                  pl.BlockSpec((HALO, D), lambda i: (jnp.maximum(i * hb - 1, 0), 0)),
                  pl.BlockSpec((HALO, D), lambda i: (jnp.minimum((i + 1) * hb, n_hb - 1), 0)),
                  pl.BlockSpec((1, 1, mod_l.shape[2]), lambda i: (i // tiles_per_seq, 0, 0))]
                 + [_const_spec(a.shape) for a in consts],
        out_specs=pl.BlockSpec((tm, D), lambda i: (i, 0)),
        out_shape=jax.ShapeDtypeStruct((T, D), F32),
        scratch_shapes=[pltpu.VMEM((tm, wd.shape[0]), BF16)],
        compiler_params=_cparams(1), name="ffn",
    )(x, x, x, mod_l, *consts)


def _pack_w_in(w_in):
    widths = [MLA_Q_RANK, MLA_KV_RANK, MLA_ROPE, RET_HEADS * RET_DK, RET_HEADS * RET_DK,
              RET_HEADS * RET_DV, RET_HEADS * RET_DV, S5_WIDTH]
    sp = [0] + [int(v) for v in np.cumsum(widths)]
    w_qc, w_kvc, w_kr, w_rq, w_rk, w_rv, w_rg, w_u = [w_in[..., sp[i]:sp[i + 1]] for i in range(8)]
    w_gates = w_in[..., sp[8]:]
    w_kr = jnp.pad(w_kr, ((0, 0), (0, 0), (MLA_NOPE, HEAD_PAD - MLA_NOPE - MLA_ROPE)))
    blocks = [w_qc, w_kvc, w_kr, w_rq * (RET_DK ** -0.5), w_rk, w_rv, w_rg, w_u, w_gates]
    return jnp.concatenate(blocks, axis=-1).astype(BF16)


def _pack_mla(w_uq, w_ukv):
    L = w_uq.shape[0]
    dq = MLA_NOPE + MLA_ROPE
    wq = w_uq.reshape(L, MLA_Q_RANK, MLA_HEADS, dq)
    wq = jnp.pad(wq, ((0, 0), (0, 0), (0, 0), (0, HEAD_PAD - dq)))
    wkv = w_ukv.reshape(L, MLA_KV_RANK, MLA_HEADS, MLA_NOPE + MLA_V)
    wk = jnp.pad(wkv[..., :MLA_NOPE], ((0, 0), (0, 0), (0, 0), (0, HEAD_PAD - MLA_NOPE)))
    wv = wkv[..., MLA_NOPE:].reshape(L, MLA_KV_RANK, MLA_HEADS * MLA_V)
    flat = lambda w: w.reshape(L, w.shape[1], MLA_HEADS * HEAD_PAD).astype(BF16)
    return flat(wq), flat(wk), wv.astype(BF16)


def _ret_tables(ret_log_decay, c):
    lg = jnp.log1p(-jnp.exp(ret_log_decay.astype(F32)))
    lg_f, lg_b = lg[:, 0, :, None, None], lg[:, 1, :, None, None]
    idx = jnp.arange(c, dtype=F32)
    diff = idx[:, None] - idx[None, :]
    dmask = jnp.where(diff >= 0, jnp.exp(lg_f * jnp.maximum(diff, 0.0)),
                      jnp.exp(lg_b * jnp.maximum(-diff, 0.0)))
    lf, lb = lg[:, 0, :, None], lg[:, 1, :, None]
    ones = jnp.ones((c,), F32)
    cols = [jnp.exp(lf * (idx + 1)), jnp.exp(lf * (c - 1 - idx)), jnp.exp(lb * (c - idx)),
            jnp.exp(lb * idx), jnp.exp(lf * c) * ones, jnp.exp(lb * c) * ones]
    vec = jnp.stack(cols, axis=2)
    return dmask, jnp.broadcast_to(vec[..., None], vec.shape + (LANES,))


def _s5_tables(lam_re, lam_im, log_step, b_re, b_im, c_re, c_im):
    L = lam_re.shape[0]
    G, P, CH, C = S5_GROUPS, S5_STATE, S5_GROUP_CH, S5_CHUNK
    f32 = lambda t: t.astype(F32)
    lam_re, lam_im, b_re, b_im, c_re, c_im = map(f32, (lam_re, lam_im, b_re, b_im, c_re, c_im))
    step = jnp.exp(f32(log_step))[..., None]
    mag = jnp.exp(lam_re * step)
    a_re, a_im = mag * jnp.cos(lam_im * step), mag * jnp.sin(lam_im * step)
    den = jnp.square(lam_re) + jnp.square(lam_im)
    f_re = ((a_re - 1.0) * lam_re + a_im * lam_im) / den
    f_im = (a_im * lam_re - (a_re - 1.0) * lam_im) / den
    bb_re = f_re[..., None] * b_re - f_im[..., None] * b_im
    bb_im = f_re[..., None] * b_im + f_im[..., None] * b_re
    k = jnp.arange(C + 1, dtype=F32)[:, None, None, None, None]
    pm = jnp.exp(k * (lam_re * step))
    pw_re, pw_im = pm * jnp.cos(k * (lam_im * step)), pm * jnp.sin(k * (lam_im * step))

    NP = S5_PAIRS
    lane = np.arange(CHUNK_W)
    tok, grp, ch = lane // PAIR_W, (lane // CH) % 2, lane % CH
    onehot = lambda idx, n: jnp.asarray(np.arange(n)[:, None] == idx[None, :], F32)
    spread = lambda t, sel: jnp.einsum('...k,kn->...n', t, sel, precision=HI)
    ch_sel = onehot(ch, CH)
    own_lanes = jnp.asarray(grp[None, :] == (np.arange(G) % 2)[:, None], F32)[:, None, :]

    def times(z, mat_re, mat_im, expo):
        sel = onehot(expo, C + 1)
        pr = spread(jnp.moveaxis(pw_re[:, :, z], 0, -1), sel) * own_lanes
        pi = spread(jnp.moveaxis(pw_im[:, :, z], 0, -1), sel) * own_lanes
        mr, mi = spread(mat_re[:, z], ch_sel), spread(mat_im[:, z], ch_sel)
        return mr * pr - mi * pi, mr * pi + mi * pr

    cr, ci = jnp.swapaxes(c_re, -1, -2), jnp.swapaxes(c_im, -1, -2)
    pairs = lambda t: t.reshape((L, NP, 2) + t.shape[2:])

    of_re, of_im = times(0, cr, ci, tok + 1)
    ob_re, ob_im = times(1, cr, ci, C - tok)
    m_out = jnp.stack([pairs(t) for t in (of_re, -of_im, ob_re, -ob_im)], axis=2)
    m_out = m_out.reshape(L * NP, 4 * 2 * P, CHUNK_W)
    if_re, if_im = times(0, bb_re, bb_im, C - 1 - tok)
    ib_re, ib_im = times(1, bb_re, bb_im, tok)
    m_in_t = jnp.stack([pairs(t) for t in (if_re, if_im, ib_re, ib_im)], axis=2)
    m_in_t = m_in_t.reshape(L * NP, 4 * 2 * P, CHUNK_W)

    def lag_row(z, expo):
        h_re, h_im = times(z, cr, ci, expo)
        bt_re, bt_im = jnp.swapaxes(bb_re[:, z], -1, -2), jnp.swapaxes(bb_im[:, z], -1, -2)
        return (jnp.einsum('lgcp,lgpn->lgcn', bt_re, h_re, precision=HI)
                - jnp.einsum('lgcp,lgpn->lgcn', bt_im, h_im, precision=HI))

    row_f, row_b = lag_row(0, tok), lag_row(1, C - 1 - tok)
    strip_f = jnp.concatenate([jnp.zeros_like(row_f), row_f], axis=-1)
    strip_b = jnp.concatenate([row_b, jnp.zeros_like(row_b)], axis=-1)
    blocks = [pairs(lax.slice_in_dim(strip_f, CHUNK_W - j * PAIR_W, 2 * CHUNK_W - j * PAIR_W, axis=-1)
                    + lax.slice_in_dim(strip_b, (C - 1 - j) * PAIR_W, (C - 1 - j) * PAIR_W + CHUNK_W, axis=-1))
              for j in range(C)]
    t_p = jnp.stack(blocks, axis=2).reshape(L * NP, CHUNK_W, CHUNK_W)

    a_vec = jnp.stack([pw_re[C, :, 0], pw_im[C, :, 0], pw_re[C, :, 1], pw_im[C, :, 1]], axis=1)
    a_vec = a_vec.reshape(L, 4, G * P)
    return t_p.astype(BF16), m_in_t.astype(BF16), m_out.astype(BF16), a_vec


def _pack_ffn(w_up, conv_w, conv_b, w_down):
    cw = jnp.concatenate([conv_w, conv_b[:, None, :]], axis=1)
    return w_up.astype(BF16), cw, w_down.astype(BF16)


def _tile(n, pref):
    t = min(pref, n)
    assert n % t == 0, (n, t)
    return t


def kernel(x, c, positions, w_in, mla_q_norm, mla_w_uq, mla_kv_norm, mla_w_ukv, ret_log_decay, s5_lam_re, s5_lam_im, s5_log_step, s5_b_re, s5_b_im, s5_c_re, s5_c_im, s5_d, s5_w_glu, w_branch_mla, w_branch_ret, w_branch_s5, w_o, ffn_w_up, ffn_conv_w, ffn_conv_b, ffn_w_down, ln1_g, ln1_b, ln2_g, ln2_b, w_ada, b_ada):
    B, S, D = x.shape
    L = w_in.shape[0]
    T = B * S
    assert S % LANES == 0 and S % (16 * S5_CHUNK) == 0 and ffn_w_up.shape[2] % (2 * FFN_CHUNK) == 0
    alpha = (2 * L) ** 0.25
    tm = _tile(S, 512)
    tps = S // tm

    cos_m, sin_m, cos_r, sin_r = _rope_tables(positions, _tile(T, 2048))
    mod = _ada_mod(c, w_ada, b_ada).reshape(L, B, 1, -1)

    w_packed = _pack_w_in(w_in)
    wq, wk, wv = _pack_mla(mla_w_uq, mla_w_ukv)
    dmask, ret_vec = _ret_tables(ret_log_decay, tm)
    t_mat, m_in, m_out, a_vec = _s5_tables(s5_lam_re, s5_lam_im, s5_log_step, s5_b_re, s5_b_im,
                                           s5_c_re, s5_c_im)
    w_up, conv_cw, w_down = _pack_ffn(ffn_w_up, ffn_conv_w, ffn_conv_b, ffn_w_down)
    bf = lambda w: w.astype(BF16)
    row2 = lambda v: v.reshape(L, 1, -1)
    q_gain, kv_gain = row2(mla_q_norm), row2(mla_kv_norm)
    d_skip = row2(s5_d)
    g1, b1, g2, b2 = row2(ln1_g), row2(ln1_b), row2(ln2_g), row2(ln2_b)
    w_glu, wbm, wbr, wbs, wo = map(bf, (s5_w_glu, w_branch_mla, w_branch_ret, w_branch_s5, w_o))
    NP = S5_PAIRS

    xf = x.reshape(T, D)
    for l in range(L):
        q, k, v, rq, rk, rv, rg, u, gates = _in_proj(xf, mod[l], w_packed[l], cos_m, sin_m, q_gain[l],
                                                     kv_gain[l], wq[l], wk[l], wv[l], tm, tps)
        o_mla = _mla_attn(q, k, v, B, S, _tile(S, 1024), _tile(S, 1024))
        o_ret = _retention(rq, rk, rv, rg, cos_r, sin_r, dmask[l], ret_vec[l], B, S, tm)
        y_ssm = _s5_core(u, m_in[l * NP:(l + 1) * NP], t_mat[l * NP:(l + 1) * NP],
                         m_out[l * NP:(l + 1) * NP], a_vec[l], B, S)
        x1 = _merge(xf, mod[l], o_mla, o_ret, y_ssm, u, gates, d_skip[l],
                    w_glu[l], wbm[l], wbr[l], wbs[l], wo[l], g1[l], b1[l], tm, tps, alpha)
        xf = _ffn(x1, mod[l], w_up[l], conv_cw[l], w_down[l], g2[l], b2[l], tm, tps, alpha)
    return xf.reshape(B, S, D)
```

```python
import functools
import math
from typing import NamedTuple

import jax
import jax.numpy as jnp
import numpy as np
from jax import lax
from jax.experimental import pallas as pl
from jax.experimental.pallas import tpu as pltpu

F32 = jnp.float32
BF16 = jnp.bfloat16
HI = lax.Precision.HIGHEST

MLA_HEADS = 8
MLA_Q_RANK = 256
MLA_KV_RANK = 128
MLA_NOPE = 64
MLA_ROPE = 32
MLA_V = 64
RET_HEADS = 4
RET_DK = 64
RET_DV = 128
S5_GROUP_CH = 16
S5_WIDTH = 512
S5_GROUPS = S5_WIDTH // S5_GROUP_CH
S5_STATE = 64
S5_CHUNK = 16
S5_PAIRS = S5_GROUPS // 2
ROPE_BASE = 10000.0
LN_EPS = 1e-5
RMS_EPS = 1e-6
GN_EPS = 1e-5
FFN_CHUNK = 256
S5_SCAN_LANES = 1024

LANES = 128
HEAD_PAD = 128
VMEM_LIMIT = 56 * 1024 * 1024


def _cparams(n_axes):
    return pltpu.CompilerParams(dimension_semantics=("arbitrary",) * n_axes,
                                vmem_limit_bytes=VMEM_LIMIT)


def _const_spec(shape):
    nd = len(shape)
    return pl.BlockSpec(shape, lambda *_: (0,) * nd)


def _layer_spec(a, l):
    nd = a.ndim - 1
    return pl.BlockSpec((None,) + a.shape[1:], lambda *_: (l,) + (0,) * nd)


def _mod_spec(mod, l, tiles_per_seq):
    return pl.BlockSpec((None, 1, 1, mod.shape[3]), lambda i: (l, i // tiles_per_seq, 0, 0))


def _layer_norm(x):
    mu = jnp.mean(x, axis=-1, keepdims=True)
    xc = x - mu
    var = jnp.mean(xc * xc, axis=-1, keepdims=True)
    return xc * lax.rsqrt(var + LN_EPS)


def _rms_norm(x, g):
    return x * lax.rsqrt(jnp.mean(x * x, axis=-1, keepdims=True) + RMS_EPS) * g


def _dot(a, b):
    return jnp.dot(a, b, preferred_element_type=F32)


def _dot_nt(a, b):
    return lax.dot_general(a, b, (((1,), (1,)), ((), ())), preferred_element_type=F32)


def _transposed(a):
    n = a.shape[1]
    eye = (lax.broadcasted_iota(jnp.int32, (n, n), 0)
           == lax.broadcasted_iota(jnp.int32, (n, n), 1)).astype(a.dtype)
    return _dot_nt(eye, a).astype(a.dtype)


def _rope_body(pos_ref, invf_ref, cm_ref, sm_ref, cr_ref, sr_ref):
    p = pos_ref[...].astype(F32)
    ang_m = p * invf_ref[0:1, :]
    ang_r = p * invf_ref[1:2, :]
    cm_ref[...] = jnp.cos(ang_m)
    sm_ref[...] = jnp.sin(ang_m)
    cr_ref[...] = jnp.cos(ang_r)
    sr_ref[...] = jnp.sin(ang_r)


def _rope_tables(positions, tm):
    T = positions.size
    half_m, half_r = MLA_ROPE // 2, RET_DK // 2
    inv_m = ROPE_BASE ** (-jnp.arange(half_m, dtype=F32) / half_m)
    inv_r = ROPE_BASE ** (-jnp.arange(half_r, dtype=F32) / half_r)
    lane = np.arange(LANES)
    row_m = jnp.where((lane >= MLA_NOPE) & (lane < MLA_NOPE + MLA_ROPE),
                      inv_m[(lane - MLA_NOPE) % half_m], 0.0)
    row_r = inv_r[lane % half_r]
    invf = jnp.stack([row_m, row_r]).astype(F32)
    out = jax.ShapeDtypeStruct((T, LANES), F32)
    spec = pl.BlockSpec((tm, LANES), lambda i: (i, 0))
    return pl.pallas_call(
        _rope_body, grid=(T // tm,),
        in_specs=[pl.BlockSpec((tm, 1), lambda i: (i, 0)), _const_spec((2, LANES))],
        out_specs=[spec] * 4, out_shape=[out] * 4,
        compiler_params=_cparams(1), name="rope_tables",
    )(positions.reshape(T, 1), invf)


def _mod_body(c_ref, w_ref, b_ref, o_ref):
    cond = jax.nn.silu(c_ref[...]).astype(BF16)
    o_ref[0] = _dot(cond, w_ref[0].astype(BF16)) + b_ref[0]


def _ada_mod(c, w_ada, b_ada):
    L, D, N = w_ada.shape
    B = c.shape[0]
    tn = D
    return pl.pallas_call(
        _mod_body, grid=(L, N // tn),
        in_specs=[_const_spec((B, D)),
                  pl.BlockSpec((1, D, tn), lambda l, j: (l, 0, j)),
                  pl.BlockSpec((1, 1, tn), lambda l, j: (l, 0, j))],
        out_specs=pl.BlockSpec((1, B, tn), lambda l, j: (l, 0, j)),
        out_shape=jax.ShapeDtypeStruct((L, B, N), F32),
        compiler_params=_cparams(2), name="ada_mod",
    )(c, w_ada, b_ada.reshape(L, 1, N))


_MLA_IN = MLA_Q_RANK + MLA_KV_RANK + HEAD_PAD
_IN_BLOCKS = (("rq", 256, BF16), ("rk", 256, BF16), ("rv", 512, BF16), ("rg", 512, BF16),
              ("u", 512, F32), ("gates", 3072, BF16))


def _in_proj_body(x_ref, mod_ref, w_ref, cos_ref, sin_ref, qg_ref, kvg_ref, wq_ref, wk_ref, wv_ref,
                  q_ref, k_ref, v_ref, *out_refs):
    D = x_ref.shape[1]
    xn = _layer_norm(x_ref[...])
    shift = mod_ref[0, :, 0:D]
    scale = mod_ref[0, :, D:2 * D]
    h = (xn * (1.0 + scale) + shift).astype(BF16)

    lat = _dot(h, w_ref[:, 0:_MLA_IN])
    cos, sin = cos_ref[...], sin_ref[...]
    qn = _rms_norm(lat[:, :MLA_Q_RANK], qg_ref[...]).astype(BF16)
    q_scale = (MLA_NOPE + MLA_ROPE) ** -0.5 * math.log2(math.e)
    q_ref[...] = (_mla_rope(_dot(qn, wq_ref[...]), cos, sin) * q_scale).astype(BF16)
    kvn = _rms_norm(lat[:, MLA_Q_RANK:MLA_Q_RANK + MLA_KV_RANK], kvg_ref[...]).astype(BF16)
    k_rope = _mla_rope(lat[:, MLA_Q_RANK + MLA_KV_RANK:], cos, sin)
    k = _dot(kvn, wk_ref[...]) + jnp.concatenate([k_rope] * MLA_HEADS, axis=1)
    k_ref[...] = k.astype(BF16)
    v_ref[...] = _dot(kvn, wv_ref[...]).astype(BF16)

    lo = _MLA_IN
    for (_, width, _), o_ref in zip(_IN_BLOCKS, out_refs):
        for c0 in range(0, width, 512):
            c1 = min(c0 + 512, width)
            o_ref[:, c0:c1] = _dot(h, w_ref[:, lo + c0:lo + c1]).astype(o_ref.dtype)
        lo += width


def _in_proj(x, l, mod, w_packed, cos_m, sin_m, q_gain, kv_gain, wq, wk, wv, tm, tiles_per_seq):
    T, D = x.shape
    row = lambda i: (i, 0)
    consts = (q_gain, kv_gain, wq, wk, wv)
    mla_out = [(MLA_HEADS * HEAD_PAD, BF16), (MLA_HEADS * HEAD_PAD, BF16), (MLA_HEADS * MLA_V, BF16)]
    outs = mla_out + [(w, dt) for _, w, dt in _IN_BLOCKS]
    return pl.pallas_call(
        _in_proj_body, grid=(T // tm,),
        in_specs=[pl.BlockSpec((tm, D), row),
                  _mod_spec(mod, l, tiles_per_seq), _layer_spec(w_packed, l),
                  pl.BlockSpec((tm, LANES), row), pl.BlockSpec((tm, LANES), row)]
                 + [_layer_spec(a, l) for a in consts],
        out_specs=[pl.BlockSpec((tm, w), row) for w, _ in outs],
        out_shape=[jax.ShapeDtypeStruct((T, w), dt) for w, dt in outs],
        compiler_params=_cparams(1), name="in_proj",
    )(x, mod, w_packed, cos_m, sin_m, *consts)


def _rot_half(x, lo, half):
    lane = lax.broadcasted_iota(jnp.int32, x.shape, 1)
    up = pltpu.roll(x, LANES - half, 1)
    down = pltpu.roll(x, half, 1)
    if lo is None:
        return jnp.where(lane % (2 * half) < half, -up, down)
    return jnp.where((lane >= lo) & (lane < lo + half), -up,
                     jnp.where((lane >= lo + half) & (lane < lo + 2 * half), down, 0.0))


def _mla_rope(x, cos, sin):
    tiles = [x[:, i * HEAD_PAD:(i + 1) * HEAD_PAD] for i in range(x.shape[1] // HEAD_PAD)]
    return jnp.concatenate([t * cos + _rot_half(t, MLA_NOPE, MLA_ROPE // 2) * sin for t in tiles],
                           axis=1)


def _attn_body(q_ref, k_ref, v_ref, o_ref, m_ref, l_ref, acc_ref, *, tk):
    tq = q_ref.shape[0]
    nk = k_ref.shape[0] // tk
    nt = tk // LANES
    heads = [slice(j * HEAD_PAD, (j + 1) * HEAD_PAD) for j in range(2)]
    m_ref[...] = jnp.full(m_ref.shape, -jnp.inf, F32)
    l_ref[...] = jnp.zeros(l_ref.shape, F32)
    acc_ref[...] = jnp.zeros(acc_ref.shape, F32)

    def kstep(t, carry):
        r0 = pl.multiple_of(t * tk, tk)
        v = v_ref[pl.ds(r0, tk), :]
        s = [_dot_nt(q_ref[:, heads[j]], k_ref[pl.ds(r0, tk), heads[j]]) for j in range(2)]
        for j in range(2):
            tiles = [s[j][:, i * LANES:(i + 1) * LANES] for i in range(nt)]
            m_prev = m_ref[j]
            m_new = jnp.maximum(m_prev, jnp.max(functools.reduce(jnp.maximum, tiles),
                                                axis=1, keepdims=True))
            alpha = jnp.exp2(m_prev - m_new)
            p = [jnp.exp2(t_ - m_new) for t_ in tiles]
            l_ref[j] = alpha * l_ref[j] + functools.reduce(jnp.add, p)
            m_ref[j] = m_new
            pb = jnp.concatenate([t_.astype(BF16) for t_ in p], axis=1)
            acc_ref[j] = alpha * acc_ref[j] + _dot(pb, v)
        return carry

    lax.fori_loop(0, nk, kstep, 0, unroll=4)
    outs = [acc_ref[j] / jnp.sum(l_ref[j], axis=1, keepdims=True) for j in range(2)]
    lane = lax.broadcasted_iota(jnp.int32, (tq, 2 * MLA_V), 1)
    o_ref[...] = jnp.where(lane < MLA_V, outs[0], outs[1]).astype(o_ref.dtype)


def _mla_attn(q, k, v, B, S, tq, tk):
    T = q.shape[0]
    nq = S // tq
    pairs = MLA_HEADS // 2
    return pl.pallas_call(
        functools.partial(_attn_body, tk=tk), grid=(B, pairs, nq),
        in_specs=[pl.BlockSpec((tq, 2 * HEAD_PAD), lambda b, h, i: (b * nq + i, h)),
                  pl.BlockSpec((S, 2 * HEAD_PAD), lambda b, h, i: (b, h)),
                  pl.BlockSpec((S, 2 * MLA_V), lambda b, h, i: (b, h))],
        out_specs=pl.BlockSpec((tq, 2 * MLA_V), lambda b, h, i: (b * nq + i, h)),
        out_shape=jax.ShapeDtypeStruct((T, MLA_HEADS * MLA_V), BF16),
        scratch_shapes=[pltpu.VMEM((2, tq, LANES), F32), pltpu.VMEM((2, tq, LANES), F32),
                        pltpu.VMEM((2, tq, 2 * MLA_V), F32)],
        compiler_params=_cparams(3), name="mla_attn",
    )(q, k, v)


_XI_F, _ZETA_F, _XI_B, _ZETA_B, _CARRY_F, _CARRY_B = range(6)


def _ret_rope(r_ref, cos, sin):
    r = r_ref[...].astype(F32)
    tiles = [r[:, i * LANES:(i + 1) * LANES] for i in range(r.shape[1] // LANES)]
    return jnp.concatenate([t * cos + _rot_half(t, None, RET_DK // 2) * sin for t in tiles], axis=1)


def _head_masked(x_pair, h):
    lane = lax.broadcasted_iota(jnp.int32, x_pair.shape, 1)
    lo = (h % 2) * RET_DK
    return jnp.where((lane >= lo) & (lane < lo + RET_DK), x_pair, 0.0)


def _ret_fwd_body(rq_ref, rk_ref, rv_ref, cos_ref, sin_ref, dm_ref, vec_ref, y_ref, s_ref):
    @pl.when(pl.program_id(1) == 0)
    def _():
        s_ref[...] = jnp.zeros(s_ref.shape, F32)

    q = _ret_rope(rq_ref, cos_ref[...], sin_ref[...])
    k = _ret_rope(rk_ref, cos_ref[...], sin_ref[...])
    for h in range(RET_HEADS):
        pair = slice((h // 2) * LANES, (h // 2 + 1) * LANES)
        vcols = slice(h * RET_DV, (h + 1) * RET_DV)
        qm = _head_masked(q[:, pair], h)
        kp = k[:, pair]
        vh = rv_ref[:, vcols]
        scores = _dot_nt(qm.astype(BF16), kp.astype(BF16)) * dm_ref[h]
        state = s_ref[h]
        y_ref[:, vcols] = (_dot(scores.astype(BF16), vh)
                           + _dot((qm * vec_ref[h, _XI_F]).astype(BF16), state.astype(BF16)))
        kz = (kp * vec_ref[h, _ZETA_F]).astype(BF16)
        s_ref[h] = vec_ref[h, _CARRY_F, 0:LANES, :] * state + _dot(_transposed(kz), vh)


def _ret_bwd_body(rq_ref, rk_ref, rv_ref, rg_ref, ya_ref, cos_ref, sin_ref, vec_ref, o_ref, s_ref):
    @pl.when(pl.program_id(1) == 0)
    def _():
        s_ref[...] = jnp.zeros(s_ref.shape, F32)

    q = _ret_rope(rq_ref, cos_ref[...], sin_ref[...])
    k = _ret_rope(rk_ref, cos_ref[...], sin_ref[...])
    for h in range(RET_HEADS):
        pair = slice((h // 2) * LANES, (h // 2 + 1) * LANES)
        vcols = slice(h * RET_DV, (h + 1) * RET_DV)
        qm = _head_masked(q[:, pair], h)
        state = s_ref[h]
        y = ya_ref[:, vcols] + _dot((qm * vec_ref[h, _XI_B]).astype(BF16), state.astype(BF16))
        kz = (k[:, pair] * vec_ref[h, _ZETA_B]).astype(BF16)
        s_ref[h] = vec_ref[h, _CARRY_B, 0:LANES, :] * state + _dot(_transposed(kz), rv_ref[:, vcols])
        mu = jnp.mean(y, axis=-1, keepdims=True)
        yc = y - mu
        var = jnp.mean(yc * yc, axis=-1, keepdims=True)
        yn = yc * lax.rsqrt(var + GN_EPS)
        gate = jax.nn.silu(rg_ref[:, vcols].astype(F32))
        o_ref[:, vcols] = (gate * yn).astype(o_ref.dtype)


def _retention(rq, rk, rv, rg, cos_r, sin_r, l, dmask, vec, B, S, tr):
    T = rq.shape[0]
    nb = S // tr
    w = RET_HEADS * RET_DV
    fwd = lambda b, j: (b * nb + j, 0)
    bwd = lambda b, j: (b * nb + nb - 1 - j, 0)
    state = pltpu.VMEM((RET_HEADS, LANES, RET_DV), F32)
    ya = pl.pallas_call(
        _ret_fwd_body, grid=(B, nb),
        in_specs=[pl.BlockSpec((tr, rq.shape[1]), fwd), pl.BlockSpec((tr, rk.shape[1]), fwd),
                  pl.BlockSpec((tr, w), fwd), pl.BlockSpec((tr, LANES), fwd),
                  pl.BlockSpec((tr, LANES), fwd), _layer_spec(dmask, l), _layer_spec(vec, l)],
        out_specs=pl.BlockSpec((tr, w), fwd),
        out_shape=jax.ShapeDtypeStruct((T, w), F32),
        scratch_shapes=[state], compiler_params=_cparams(2), name="ret_fwd",
    )(rq, rk, rv, cos_r, sin_r, dmask, vec)
    return pl.pallas_call(
        _ret_bwd_body, grid=(B, nb),
        in_specs=[pl.BlockSpec((tr, rq.shape[1]), bwd), pl.BlockSpec((tr, rk.shape[1]), bwd),
                  pl.BlockSpec((tr, w), bwd), pl.BlockSpec((tr, w), bwd), pl.BlockSpec((tr, w), bwd),
                  pl.BlockSpec((tr, LANES), bwd), pl.BlockSpec((tr, LANES), bwd),
                  _layer_spec(vec, l)],
        out_specs=pl.BlockSpec((tr, w), bwd),
        out_shape=jax.ShapeDtypeStruct((T, w), BF16),
        scratch_shapes=[state], compiler_params=_cparams(2), name="ret_bwd",
    )(rq, rk, rv, rg, ya, cos_r, sin_r, vec)


PAIR_W = 2 * S5_GROUP_CH
PAIRS_PER_TILE = LANES // PAIR_W
CHUNK_W = S5_CHUNK * PAIR_W


def _chunk_rows(u_ref, n):
    tok = [u_ref[pl.ds(i, n, stride=S5_CHUNK), :] for i in range(S5_CHUNK)]
    return [jnp.concatenate([t[:, p * PAIR_W:(p + 1) * PAIR_W] for t in tok], axis=1)
            for p in range(PAIRS_PER_TILE)]


def _s5_in_body(u_ref, min_ref, fr_ref, fi_ref, br_ref, bi_ref):
    n = fr_ref.shape[1]
    for p, rows in enumerate(_chunk_rows(u_ref, n)):
        v = _dot_nt(rows.astype(BF16), min_ref[p])
        for part, ref in enumerate((fr_ref, fi_ref, br_ref, bi_ref)):
            ref[0, :, p * LANES:(p + 1) * LANES] = v[:, part * LANES:(part + 1) * LANES]


def _s5_scan_body(fr_ref, fi_ref, br_ref, bi_ref, a_ref, zfr_ref, zfi_ref, zbr_ref, zbi_ref, *, rb):
    n = fr_ref.shape[1]
    nblk = n // rb
    lw = fr_ref.shape[2]
    afr, afi = a_ref[0:1, :], a_ref[1:2, :]
    abr, abi = a_ref[2:3, :], a_ref[3:4, :]
    zero = jnp.zeros((1, lw), F32)

    def step(blk, carry):
        xr, xi, yr, yi = carry
        r0 = pl.multiple_of(blk * rb, rb)
        vr, vi = fr_ref[0, pl.ds(r0, rb), :], fi_ref[0, pl.ds(r0, rb), :]
        out_r, out_i = [], []
        for r in range(rb):
            out_r.append(xr)
            out_i.append(xi)
            xr, xi = (afr * xr - afi * xi + vr[r:r + 1], afr * xi + afi * xr + vi[r:r + 1])
        zfr_ref[0, pl.ds(r0, rb), :] = jnp.concatenate(out_r, axis=0).astype(zfr_ref.dtype)
        zfi_ref[0, pl.ds(r0, rb), :] = jnp.concatenate(out_i, axis=0).astype(zfi_ref.dtype)
        r1 = pl.multiple_of((nblk - 1 - blk) * rb, rb)
        wr, wi = br_ref[0, pl.ds(r1, rb), :], bi_ref[0, pl.ds(r1, rb), :]
        out_r, out_i = [None] * rb, [None] * rb
        for r in reversed(range(rb)):
            out_r[r] = yr
            out_i[r] = yi
            yr, yi = (abr * yr - abi * yi + wr[r:r + 1], abr * yi + abi * yr + wi[r:r + 1])
        zbr_ref[0, pl.ds(r1, rb), :] = jnp.concatenate(out_r, axis=0).astype(zbr_ref.dtype)
        zbi_ref[0, pl.ds(r1, rb), :] = jnp.concatenate(out_i, axis=0).astype(zbi_ref.dtype)
        return xr, xi, yr, yi

    lax.fori_loop(0, nblk, step, (zero, zero, zero, zero))


def _s5_out_body(u_ref, zfr_ref, zfi_ref, zbr_ref, zbi_ref, t_ref, mout_ref, y_ref):
    n = zfr_ref.shape[1]
    ys = []
    for p, rows in enumerate(_chunk_rows(u_ref, n)):
        cols = slice(p * LANES, (p + 1) * LANES)
        z = jnp.concatenate([r[0, :, cols] for r in (zfr_ref, zfi_ref, zbr_ref, zbi_ref)], axis=1)
        ys.append(_dot(rows.astype(BF16), t_ref[p]) + _dot(z, mout_ref[p]))
    for i in range(S5_CHUNK):
        y_ref[pl.ds(i, n, stride=S5_CHUNK), :] = jnp.concatenate(
            [y[:, i * PAIR_W:(i + 1) * PAIR_W] for y in ys], axis=1)


def _s5_core(u, l, m_in, t_mat, m_out, a_vec, B, S):
    N = S // S5_CHUNK
    n_tiles = S5_WIDTH // LANES
    sw = S5_GROUPS * S5_STATE
    pw = PAIRS_PER_TILE * LANES
    u_spec = pl.BlockSpec((S, LANES), lambda b, q: (b, q))
    part_spec = pl.BlockSpec((1, N, pw), lambda b, q: (b, 0, q))
    mat_spec = pl.BlockSpec((None, PAIRS_PER_TILE, CHUNK_W, CHUNK_W), lambda b, q: (l, q, 0, 0))
    parts = pl.pallas_call(
        _s5_in_body, grid=(B, n_tiles),
        in_specs=[u_spec, mat_spec], out_specs=[part_spec] * 4,
        out_shape=[jax.ShapeDtypeStruct((B, N, sw), F32)] * 4,
        compiler_params=_cparams(2), name="s5_in",
    )(u, m_in)
    lw = min(S5_SCAN_LANES, sw)
    blk = pl.BlockSpec((1, N, lw), lambda b, j: (b, 0, j))
    z = pl.pallas_call(
        functools.partial(_s5_scan_body, rb=16), grid=(B, sw // lw),
        in_specs=[blk] * 4 + [pl.BlockSpec((None, 4, lw), lambda b, j: (l, 0, j))],
        out_specs=[blk] * 4, out_shape=[jax.ShapeDtypeStruct((B, N, sw), BF16)] * 4,
        compiler_params=_cparams(2), name="s5_scan",
    )(*parts, a_vec)
    return pl.pallas_call(
        _s5_out_body, grid=(B, n_tiles),
        in_specs=[u_spec] + [part_spec] * 4 + [mat_spec, mat_spec],
        out_specs=u_spec, out_shape=jax.ShapeDtypeStruct((B * S, S5_WIDTH), F32),
        compiler_params=_cparams(2), name="s5_out",
    )(u, *z, t_mat, m_out)


def _merge_body(x_ref, mod_ref, om_ref, or_ref, ys_ref, u_ref, g_ref, d_ref, wglu_ref,
                wm_ref, wr_ref, ws_ref, wo_ref, lng_ref, lnb_ref, o_ref, *, alpha):
    D = x_ref.shape[1]
    y = d_ref[...] * u_ref[...].astype(F32) + ys_ref[...].astype(F32)
    y = jax.nn.gelu(y)
    o_s5 = y * jax.nn.sigmoid(_dot(y.astype(BF16), wglu_ref[...]))
    merged = (jax.nn.sigmoid(g_ref[:, 0:D].astype(F32)) * _dot(om_ref[...], wm_ref[...])
              + jax.nn.sigmoid(g_ref[:, D:2 * D].astype(F32)) * _dot(or_ref[...], wr_ref[...])
              + jax.nn.sigmoid(g_ref[:, 2 * D:3 * D].astype(F32)) * _dot(o_s5.astype(BF16), ws_ref[...]))
    out = _dot(merged.astype(BF16), wo_ref[...])
    gate1 = mod_ref[0, :, 2 * D:3 * D]
    o_ref[...] = _layer_norm(alpha * x_ref[...] + gate1 * out) * lng_ref[...] + lnb_ref[...]


def _merge(x, l, mod, o_mla, o_ret, y_ssm, u, gates, d_skip, w_glu, wm, wr, ws, wo, ln_g, ln_b,
           tm, tiles_per_seq, alpha):
    T, D = x.shape
    row = lambda i: (i, 0)
    consts = (d_skip, w_glu, wm, wr, ws, wo, ln_g, ln_b)
    return pl.pallas_call(
        functools.partial(_merge_body, alpha=alpha), grid=(T // tm,),
        in_specs=[pl.BlockSpec((tm, D), row), _mod_spec(mod, l, tiles_per_seq)]
                 + [pl.BlockSpec((tm, a.shape[1]), row) for a in (o_mla, o_ret, y_ssm, u, gates)]
                 + [_layer_spec(a, l) for a in consts],
        out_specs=pl.BlockSpec((tm, D), row),
        out_shape=jax.ShapeDtypeStruct((T, D), F32),
        compiler_params=_cparams(1), name="merge",
    )(x, mod, o_mla, o_ret, y_ssm, u, gates, *consts)


HALO = 8


def _ffn_body(x_ref, xp_ref, xn_ref, mod_ref, wup_ref, cw_ref, wd_ref,
              lng_ref, lnb_ref, o_ref, act_ref, *, alpha, tiles_per_seq):
    tm, D = x_ref.shape
    dff = wd_ref.shape[0]
    i = pl.program_id(0)
    shift = mod_ref[0, :, 3 * D:4 * D]
    scale = mod_ref[0, :, 4 * D:5 * D]
    gate2 = mod_ref[0, :, 5 * D:6 * D]
    x = x_ref[...]
    x_ext = jnp.concatenate([xp_ref[...], x, xn_ref[...]], axis=0)
    h = _layer_norm(x_ext) * (1.0 + scale) + shift
    r = lax.broadcasted_iota(jnp.int32, (tm + 2 * HALO, 1), 0)
    pos_in_seq = i % tiles_per_seq
    lo = jnp.where(pos_in_seq == 0, HALO, 0)
    hi = jnp.where(pos_in_seq == tiles_per_seq - 1, tm + HALO, tm + 2 * HALO)
    h = jnp.where((r >= lo) & (r < hi), h, 0.0).astype(BF16)
    n_ext = tm + 2 * HALO

    def conv_up(c0):
        cols = slice(c0, c0 + FFN_CHUNK)
        up = _dot(h, wup_ref[:, cols])
        prev = pltpu.roll(up, 1, 0)
        nxt = pltpu.roll(up, n_ext - 1, 0)
        y = (prev * cw_ref[0:1, cols] + up * cw_ref[1:2, cols] + nxt * cw_ref[2:3, cols]
             + cw_ref[3:4, cols])
        return y[HALO:HALO + tm, :]

    for c0 in range(0, dff, FFN_CHUNK):
        act_ref[:, c0:c0 + FFN_CHUNK] = (jax.nn.silu(conv_up(dff + c0)) * conv_up(c0)).astype(BF16)
    f = _dot(act_ref[...], wd_ref[...])
    o_ref[...] = _layer_norm(alpha * x + gate2 * f) * lng_ref[...] + lnb_ref[...]


def _ffn(x, l, mod, w_up, cw, wd, ln_g, ln_b, tm, tiles_per_seq, alpha):
    T, D = x.shape
    hb = tm // HALO
    n_hb = T // HALO
    consts = (w_up, cw, wd, ln_g, ln_b)
    return pl.pallas_call(
        functools.partial(_ffn_body, alpha=alpha, tiles_per_seq=tiles_per_seq), grid=(T // tm,),
        in_specs=[pl.BlockSpec((tm, D), lambda i: (i, 0)),
                  pl.BlockSpec((HALO, D), lambda i: (jnp.maximum(i * hb - 1, 0), 0)),
                  pl.BlockSpec((HALO, D), lambda i: (jnp.minimum((i + 1) * hb, n_hb - 1), 0)),
                  _mod_spec(mod, l, tiles_per_seq)]
                 + [_layer_spec(a, l) for a in consts],
        out_specs=pl.BlockSpec((tm, D), lambda i: (i, 0)),
        out_shape=jax.ShapeDtypeStruct((T, D), F32),
        scratch_shapes=[pltpu.VMEM((tm, wd.shape[1]), BF16)],
        compiler_params=_cparams(1), name="ffn",
    )(x, x, x, mod, *consts)


def _pack_w_in(w_in):
    widths = [MLA_Q_RANK, MLA_KV_RANK, MLA_ROPE, RET_HEADS * RET_DK, RET_HEADS * RET_DK,
              RET_HEADS * RET_DV, RET_HEADS * RET_DV, S5_WIDTH]
    sp = [0] + [int(v) for v in np.cumsum(widths)]
    w_qc, w_kvc, w_kr, w_rq, w_rk, w_rv, w_rg, w_u = [w_in[..., sp[i]:sp[i + 1]] for i in range(8)]
    w_gates = w_in[..., sp[8]:]
    w_kr = jnp.pad(w_kr, ((0, 0), (0, 0), (MLA_NOPE, HEAD_PAD - MLA_NOPE - MLA_ROPE)))
    blocks = [w_qc, w_kvc, w_kr, w_rq * (RET_DK ** -0.5), w_rk, w_rv, w_rg, w_u, w_gates]
    return jnp.concatenate(blocks, axis=-1).astype(BF16)


def _pack_mla(w_uq, w_ukv):
    L = w_uq.shape[0]
    dq = MLA_NOPE + MLA_ROPE
    wq = w_uq.reshape(L, MLA_Q_RANK, MLA_HEADS, dq)
    wq = jnp.pad(wq, ((0, 0), (0, 0), (0, 0), (0, HEAD_PAD - dq)))
    wkv = w_ukv.reshape(L, MLA_KV_RANK, MLA_HEADS, MLA_NOPE + MLA_V)
    wk = jnp.pad(wkv[..., :MLA_NOPE], ((0, 0), (0, 0), (0, 0), (0, HEAD_PAD - MLA_NOPE)))
    wv = wkv[..., MLA_NOPE:].reshape(L, MLA_KV_RANK, MLA_HEADS * MLA_V)
    flat = lambda w: w.reshape(L, w.shape[1], MLA_HEADS * HEAD_PAD).astype(BF16)
    return flat(wq), flat(wk), wv.astype(BF16)


def _ret_tables(ret_log_decay, c):
    lg = jnp.log1p(-jnp.exp(ret_log_decay.astype(F32)))
    lg_f, lg_b = lg[:, 0, :, None, None], lg[:, 1, :, None, None]
    idx = jnp.arange(c, dtype=F32)
    diff = idx[:, None] - idx[None, :]
    dmask = jnp.where(diff >= 0, jnp.exp(lg_f * jnp.maximum(diff, 0.0)),
                      jnp.exp(lg_b * jnp.maximum(-diff, 0.0)))
    lf, lb = lg[:, 0, :, None], lg[:, 1, :, None]
    ones = jnp.ones((c,), F32)
    cols = [jnp.exp(lf * (idx + 1)), jnp.exp(lf * (c - 1 - idx)), jnp.exp(lb * (c - idx)),
            jnp.exp(lb * idx), jnp.exp(lf * c) * ones, jnp.exp(lb * c) * ones]
    vec = jnp.stack(cols, axis=2)
    return dmask, jnp.broadcast_to(vec[..., None], vec.shape + (LANES,))


def _s5_tables(lam_re, lam_im, log_step, b_re, b_im, c_re, c_im):
    L = lam_re.shape[0]
    G, P, CH, C = S5_GROUPS, S5_STATE, S5_GROUP_CH, S5_CHUNK
    f32 = lambda t: t.astype(F32)
    lam_re, lam_im, b_re, b_im, c_re, c_im = map(f32, (lam_re, lam_im, b_re, b_im, c_re, c_im))
    step = jnp.exp(f32(log_step))[..., None]
    mag = jnp.exp(lam_re * step)
    a_re, a_im = mag * jnp.cos(lam_im * step), mag * jnp.sin(lam_im * step)
    den = jnp.square(lam_re) + jnp.square(lam_im)
    f_re = ((a_re - 1.0) * lam_re + a_im * lam_im) / den
    f_im = (a_im * lam_re - (a_re - 1.0) * lam_im) / den
    bb_re = f_re[..., None] * b_re - f_im[..., None] * b_im
    bb_im = f_re[..., None] * b_im + f_im[..., None] * b_re
    k = jnp.arange(C + 1, dtype=F32)[:, None, None, None, None]
    pm = jnp.exp(k * (lam_re * step))
    pw_re, pw_im = pm * jnp.cos(k * (lam_im * step)), pm * jnp.sin(k * (lam_im * step))

    NP = S5_PAIRS
    lane = np.arange(CHUNK_W)
    tok, grp, ch = lane // PAIR_W, (lane // CH) % 2, lane % CH
    onehot = lambda idx, n: jnp.asarray(np.arange(n)[:, None] == idx[None, :], F32)
    spread = lambda t, sel: jnp.einsum('...k,kn->...n', t, sel, precision=HI)
    ch_sel = onehot(ch, CH)
    own_lanes = jnp.asarray(grp[None, :] == (np.arange(G) % 2)[:, None], F32)[:, None, :]

    def times(z, mat_re, mat_im, expo):
        sel = onehot(expo, C + 1)
        pr = spread(jnp.moveaxis(pw_re[:, :, z], 0, -1), sel) * own_lanes
        pi = spread(jnp.moveaxis(pw_im[:, :, z], 0, -1), sel) * own_lanes
        mr, mi = spread(mat_re[:, z], ch_sel), spread(mat_im[:, z], ch_sel)
        return mr * pr - mi * pi, mr * pi + mi * pr

    cr, ci = jnp.swapaxes(c_re, -1, -2), jnp.swapaxes(c_im, -1, -2)
    pairs = lambda t: t.reshape((L, NP, 2) + t.shape[2:])

    of_re, of_im = times(0, cr, ci, tok + 1)
    ob_re, ob_im = times(1, cr, ci, C - tok)
    m_out = jnp.stack([pairs(t) for t in (of_re, -of_im, ob_re, -ob_im)], axis=2)
    m_out = m_out.reshape(L, NP, 4 * 2 * P, CHUNK_W)
    if_re, if_im = times(0, bb_re, bb_im, C - 1 - tok)
    ib_re, ib_im = times(1, bb_re, bb_im, tok)
    m_in_t = jnp.stack([pairs(t) for t in (if_re, if_im, ib_re, ib_im)], axis=2)
    m_in_t = m_in_t.reshape(L, NP, 4 * 2 * P, CHUNK_W)

    def lag_row(z, expo):
        h_re, h_im = times(z, cr, ci, expo)
        bt_re, bt_im = jnp.swapaxes(bb_re[:, z], -1, -2), jnp.swapaxes(bb_im[:, z], -1, -2)
        return (jnp.einsum('lgcp,lgpn->lgcn', bt_re, h_re, precision=HI)
                - jnp.einsum('lgcp,lgpn->lgcn', bt_im, h_im, precision=HI))

    row_f, row_b = lag_row(0, tok), lag_row(1, C - 1 - tok)
    strip_f = jnp.concatenate([jnp.zeros_like(row_f), row_f], axis=-1)
    strip_b = jnp.concatenate([row_b, jnp.zeros_like(row_b)], axis=-1)
    blocks = [pairs(lax.slice_in_dim(strip_f, CHUNK_W - j * PAIR_W, 2 * CHUNK_W - j * PAIR_W, axis=-1)
                    + lax.slice_in_dim(strip_b, (C - 1 - j) * PAIR_W, (C - 1 - j) * PAIR_W + CHUNK_W, axis=-1))
              for j in range(C)]
    t_p = jnp.stack(blocks, axis=2).reshape(L, NP, CHUNK_W, CHUNK_W)

    a_vec = jnp.stack([pw_re[C, :, 0], pw_im[C, :, 0], pw_re[C, :, 1], pw_im[C, :, 1]], axis=1)
    a_vec = a_vec.reshape(L, 4, G * P)
    return t_p.astype(BF16), m_in_t.astype(BF16), m_out.astype(BF16), a_vec


def _pack_ffn(w_up, conv_w, conv_b, w_down):
    cw = jnp.concatenate([conv_w, conv_b[:, None, :]], axis=1)
    return w_up.astype(BF16), cw, w_down.astype(BF16)


def _tile(n, pref):
    t = min(pref, n)
    assert n % t == 0, (n, t)
    return t


class _Tiles(NamedTuple):
    rows: int
    attn_q: int
    attn_k: int
    rope_rows: int


def _tiles(S, T):
    return _Tiles(rows=_tile(S, 512), attn_q=_tile(S, 2048), attn_k=_tile(S, 1024),
                  rope_rows=_tile(T, 2048))


def kernel(x, c, positions, w_in, mla_q_norm, mla_w_uq, mla_kv_norm, mla_w_ukv, ret_log_decay, s5_lam_re, s5_lam_im, s5_log_step, s5_b_re, s5_b_im, s5_c_re, s5_c_im, s5_d, s5_w_glu, w_branch_mla, w_branch_ret, w_branch_s5, w_o, ffn_w_up, ffn_conv_w, ffn_conv_b, ffn_w_down, ln1_g, ln1_b, ln2_g, ln2_b, w_ada, b_ada):
    B, S, D = x.shape
    L = w_in.shape[0]
    T = B * S
    assert S % LANES == 0 and S % (16 * S5_CHUNK) == 0 and ffn_w_up.shape[2] % (2 * FFN_CHUNK) == 0
    alpha = (2 * L) ** 0.25
    tiles = _tiles(S, T)
    tm = tiles.rows
    tps = S // tm

    cos_m, sin_m, cos_r, sin_r = _rope_tables(positions, tiles.rope_rows)
    mod = _ada_mod(c, w_ada, b_ada).reshape(L, B, 1, -1)

    w_packed = _pack_w_in(w_in)
    wq, wk, wv = _pack_mla(mla_w_uq, mla_w_ukv)
    dmask, ret_vec = _ret_tables(ret_log_decay, tm)
    t_mat, m_in, m_out, a_vec = _s5_tables(s5_lam_re, s5_lam_im, s5_log_step, s5_b_re, s5_b_im,
                                           s5_c_re, s5_c_im)
    w_up, conv_cw, w_down = _pack_ffn(ffn_w_up, ffn_conv_w, ffn_conv_b, ffn_w_down)
    bf = lambda w: w.astype(BF16)
    row2 = lambda v: v.reshape(L, 1, -1)
    q_gain, kv_gain = row2(mla_q_norm), row2(mla_kv_norm)
    d_skip = row2(s5_d)
    g1, b1, g2, b2 = row2(ln1_g), row2(ln1_b), row2(ln2_g), row2(ln2_b)
    w_glu, wbm, wbr, wbs, wo = map(bf, (s5_w_glu, w_branch_mla, w_branch_ret, w_branch_s5, w_o))
    xf = x.reshape(T, D)
    for l in range(L):
        q, k, v, rq, rk, rv, rg, u, gates = _in_proj(xf, l, mod, w_packed, cos_m, sin_m, q_gain, kv_gain,
                                                     wq, wk, wv, tm, tps)
        o_mla = _mla_attn(q, k, v, B, S, tiles.attn_q, tiles.attn_k)
        o_ret = _retention(rq, rk, rv, rg, cos_r, sin_r, l, dmask, ret_vec, B, S, tm)
        y_ssm = _s5_core(u, l, m_in, t_mat, m_out, a_vec, B, S)
        x1 = _merge(xf, l, mod, o_mla, o_ret, y_ssm, u, gates, d_skip, w_glu, wbm, wbr, wbs, wo, g1, b1,
                    tm, tps, alpha)
        xf = _ffn(x1, l, mod, w_up, conv_cw, w_down, g2, b2, tm, tps, alpha)
    return xf.reshape(B, S, D)
```

```python
import functools
import math
from typing import NamedTuple

import jax
import jax.numpy as jnp
import numpy as np
from jax import lax
from jax.experimental import pallas as pl
from jax.experimental.pallas import tpu as pltpu

F32 = jnp.float32
BF16 = jnp.bfloat16
HI = lax.Precision.HIGHEST

MLA_HEADS = 8
MLA_Q_RANK = 256
MLA_KV_RANK = 128
MLA_NOPE = 64
MLA_ROPE = 32
MLA_V = 64
RET_HEADS = 4
RET_DK = 64
RET_DV = 128
S5_GROUP_CH = 16
S5_WIDTH = 512
S5_GROUPS = S5_WIDTH // S5_GROUP_CH
S5_STATE = 64
S5_CHUNK = 16
S5_PAIRS = S5_GROUPS // 2
ROPE_BASE = 10000.0
LN_EPS = 1e-5
RMS_EPS = 1e-6
GN_EPS = 1e-5
FFN_CHUNK = 256
S5_SCAN_LANES = 1024

LANES = 128
HEAD_PAD = 128
VMEM_LIMIT = 56 * 1024 * 1024


def _cparams(n_axes):
    return pltpu.CompilerParams(dimension_semantics=("arbitrary",) * n_axes,
                                vmem_limit_bytes=VMEM_LIMIT)


def _const_spec(shape):
    nd = len(shape)
    return pl.BlockSpec(shape, lambda *_: (0,) * nd)


def _layer_spec(a, l):
    nd = a.ndim - 1
    return pl.BlockSpec((None,) + a.shape[1:], lambda *_: (l,) + (0,) * nd)


def _mod_spec(mod, l, tiles_per_seq):
    return pl.BlockSpec((None, 1, 1, mod.shape[3]), lambda i: (l, i // tiles_per_seq, 0, 0))


def _layer_norm(x):
    mu = jnp.mean(x, axis=-1, keepdims=True)
    xc = x - mu
    var = jnp.mean(xc * xc, axis=-1, keepdims=True)
    return xc * lax.rsqrt(var + LN_EPS)


def _rms_norm(x, g):
    return x * lax.rsqrt(jnp.mean(x * x, axis=-1, keepdims=True) + RMS_EPS) * g


def _dot(a, b):
    return jnp.dot(a, b, preferred_element_type=F32)


def _dot_nt(a, b):
    return lax.dot_general(a, b, (((1,), (1,)), ((), ())), preferred_element_type=F32)


def _transposed(a):
    n = a.shape[1]
    eye = (lax.broadcasted_iota(jnp.int32, (n, n), 0)
           == lax.broadcasted_iota(jnp.int32, (n, n), 1)).astype(a.dtype)
    return _dot_nt(eye, a).astype(a.dtype)


def _rope_body(pos_ref, invf_ref, cm_ref, sm_ref, cr_ref, sr_ref):
    for r in range(pos_ref.shape[0]):
        row = pos_ref[r:r + 1, :].astype(F32)
        p = jnp.broadcast_to(row, (LANES, LANES)).T
        rows = slice(r * LANES, (r + 1) * LANES)
        ang_m = p * invf_ref[0:1, :]
        ang_r = p * invf_ref[1:2, :]
        cm_ref[rows, :] = jnp.cos(ang_m)
        sm_ref[rows, :] = jnp.sin(ang_m)
        cr_ref[rows, :] = jnp.cos(ang_r)
        sr_ref[rows, :] = jnp.sin(ang_r)


def _rope_tables(positions, tm):
    T = positions.size
    half_m, half_r = MLA_ROPE // 2, RET_DK // 2
    inv_m = ROPE_BASE ** (-jnp.arange(half_m, dtype=F32) / half_m)
    inv_r = ROPE_BASE ** (-jnp.arange(half_r, dtype=F32) / half_r)
    lane = np.arange(LANES)
    row_m = jnp.where((lane >= MLA_NOPE) & (lane < MLA_NOPE + MLA_ROPE),
                      inv_m[(lane - MLA_NOPE) % half_m], 0.0)
    row_r = inv_r[lane % half_r]
    invf = jnp.stack([row_m, row_r]).astype(F32)
    out = jax.ShapeDtypeStruct((T, LANES), F32)
    spec = pl.BlockSpec((tm, LANES), lambda i: (i, 0))
    return pl.pallas_call(
        _rope_body, grid=(T // tm,),
        in_specs=[pl.BlockSpec((tm // LANES, LANES), lambda i: (i, 0)), _const_spec((2, LANES))],
        out_specs=[spec] * 4, out_shape=[out] * 4,
        compiler_params=_cparams(1), name="rope_tables",
    )(positions.reshape(T // LANES, LANES), invf)


def _mod_body(c_ref, w_ref, b_ref, o_ref):
    cond = jax.nn.silu(c_ref[...]).astype(BF16)
    o_ref[0] = _dot(cond, w_ref[0].astype(BF16)) + b_ref[0]


def _ada_mod(c, w_ada, b_ada):
    L, D, N = w_ada.shape
    B = c.shape[0]
    tn = D
    return pl.pallas_call(
        _mod_body, grid=(L, N // tn),
        in_specs=[_const_spec((B, D)),
                  pl.BlockSpec((1, D, tn), lambda l, j: (l, 0, j)),
                  pl.BlockSpec((1, 1, tn), lambda l, j: (l, 0, j))],
        out_specs=pl.BlockSpec((1, B, tn), lambda l, j: (l, 0, j)),
        out_shape=jax.ShapeDtypeStruct((L, B, N), F32),
        compiler_params=_cparams(2), name="ada_mod",
    )(c, w_ada, b_ada.reshape(L, 1, N))


_MLA_IN = MLA_Q_RANK + MLA_KV_RANK + HEAD_PAD
_IN_BLOCKS = (("rq", 256, BF16), ("rk", 256, BF16), ("rv", 512, BF16), ("rg", 512, BF16),
              ("u", 512, F32), ("gates", 3072, BF16))


def _in_proj_body(x_ref, mod_ref, w_ref, cos_ref, sin_ref, qg_ref, kvg_ref, wq_ref, wk_ref, wv_ref,
                  q_ref, k_ref, v_ref, *out_refs):
    D = x_ref.shape[1]
    xn = _layer_norm(x_ref[...])
    shift = mod_ref[0, :, 0:D]
    scale = mod_ref[0, :, D:2 * D]
    h = (xn * (1.0 + scale) + shift).astype(BF16)

    lat = _dot(h, w_ref[:, 0:_MLA_IN])
    cos, sin = cos_ref[...], sin_ref[...]
    qn = _rms_norm(lat[:, :MLA_Q_RANK], qg_ref[...]).astype(BF16)
    q_scale = (MLA_NOPE + MLA_ROPE) ** -0.5 * math.log2(math.e)
    q_ref[...] = (_mla_rope(_dot(qn, wq_ref[...]), cos, sin) * q_scale).astype(BF16)
    kvn = _rms_norm(lat[:, MLA_Q_RANK:MLA_Q_RANK + MLA_KV_RANK], kvg_ref[...]).astype(BF16)
    k_rope = _mla_rope(lat[:, MLA_Q_RANK + MLA_KV_RANK:], cos, sin)
    k = _dot(kvn, wk_ref[...]) + jnp.concatenate([k_rope] * MLA_HEADS, axis=1)
    k_ref[...] = k.astype(BF16)
    v_ref[...] = _dot(kvn, wv_ref[...]).astype(BF16)

    lo = _MLA_IN
    for (_, width, _), o_ref in zip(_IN_BLOCKS, out_refs):
        for c0 in range(0, width, 512):
            c1 = min(c0 + 512, width)
            o_ref[:, c0:c1] = _dot(h, w_ref[:, lo + c0:lo + c1]).astype(o_ref.dtype)
        lo += width


def _in_proj(x, l, mod, w_packed, cos_m, sin_m, q_gain, kv_gain, wq, wk, wv, tm, tiles_per_seq):
    T, D = x.shape
    row = lambda i: (i, 0)
    consts = (q_gain, kv_gain, wq, wk, wv)
    mla_out = [(MLA_HEADS * HEAD_PAD, BF16), (MLA_HEADS * HEAD_PAD, BF16), (MLA_HEADS * MLA_V, BF16)]
    outs = mla_out + [(w, dt) for _, w, dt in _IN_BLOCKS]
    return pl.pallas_call(
        _in_proj_body, grid=(T // tm,),
        in_specs=[pl.BlockSpec((tm, D), row),
                  _mod_spec(mod, l, tiles_per_seq), _layer_spec(w_packed, l),
                  pl.BlockSpec((tm, LANES), row), pl.BlockSpec((tm, LANES), row)]
                 + [_layer_spec(a, l) for a in consts],
        out_specs=[pl.BlockSpec((tm, w), row) for w, _ in outs],
        out_shape=[jax.ShapeDtypeStruct((T, w), dt) for w, dt in outs],
        compiler_params=_cparams(1), name="in_proj",
    )(x, mod, w_packed, cos_m, sin_m, *consts)


def _rot_half(x, lo, half):
    lane = lax.broadcasted_iota(jnp.int32, x.shape, 1)
    up = pltpu.roll(x, LANES - half, 1)
    down = pltpu.roll(x, half, 1)
    if lo is None:
        return jnp.where(lane % (2 * half) < half, -up, down)
    return jnp.where((lane >= lo) & (lane < lo + half), -up,
                     jnp.where((lane >= lo + half) & (lane < lo + 2 * half), down, 0.0))


def _mla_rope(x, cos, sin):
    tiles = [x[:, i * HEAD_PAD:(i + 1) * HEAD_PAD] for i in range(x.shape[1] // HEAD_PAD)]
    return jnp.concatenate([t * cos + _rot_half(t, MLA_NOPE, MLA_ROPE // 2) * sin for t in tiles],
                           axis=1)


def _attn_body(q_ref, k_ref, v_ref, o_ref, m_ref, l_ref, acc_ref, *, tk):
    tq = q_ref.shape[0]
    nk = k_ref.shape[0] // tk
    nt = tk // LANES
    heads = [slice(j * HEAD_PAD, (j + 1) * HEAD_PAD) for j in range(2)]
    m_ref[...] = jnp.full(m_ref.shape, -jnp.inf, F32)
    l_ref[...] = jnp.zeros(l_ref.shape, F32)
    acc_ref[...] = jnp.zeros(acc_ref.shape, F32)

    def kstep(t, carry):
        r0 = pl.multiple_of(t * tk, tk)
        v = v_ref[pl.ds(r0, tk), :]
        s = [_dot_nt(q_ref[:, heads[j]], k_ref[pl.ds(r0, tk), heads[j]]) for j in range(2)]
        for j in range(2):
            tiles = [s[j][:, i * LANES:(i + 1) * LANES] for i in range(nt)]
            m_prev = m_ref[j]
            m_new = jnp.maximum(m_prev, jnp.max(functools.reduce(jnp.maximum, tiles),
                                                axis=1, keepdims=True))
            alpha = jnp.exp2(m_prev - m_new)
            p = [jnp.exp2(t_ - m_new) for t_ in tiles]
            l_ref[j] = alpha * l_ref[j] + functools.reduce(jnp.add, p)
            m_ref[j] = m_new
            pb = jnp.concatenate([t_.astype(BF16) for t_ in p], axis=1)
            acc_ref[j] = alpha * acc_ref[j] + _dot(pb, v)
        return carry

    lax.fori_loop(0, nk, kstep, 0, unroll=4)
    outs = [acc_ref[j] / jnp.sum(l_ref[j], axis=1, keepdims=True) for j in range(2)]
    lane = lax.broadcasted_iota(jnp.int32, (tq, 2 * MLA_V), 1)
    o_ref[...] = jnp.where(lane < MLA_V, outs[0], outs[1]).astype(o_ref.dtype)


def _mla_attn(q, k, v, B, S, tq, tk):
    T = q.shape[0]
    nq = S // tq
    pairs = MLA_HEADS // 2
    return pl.pallas_call(
        functools.partial(_attn_body, tk=tk), grid=(B, pairs, nq),
        in_specs=[pl.BlockSpec((tq, 2 * HEAD_PAD), lambda b, h, i: (b * nq + i, h)),
                  pl.BlockSpec((S, 2 * HEAD_PAD), lambda b, h, i: (b, h)),
                  pl.BlockSpec((S, 2 * MLA_V), lambda b, h, i: (b, h))],
        out_specs=pl.BlockSpec((tq, 2 * MLA_V), lambda b, h, i: (b * nq + i, h)),
        out_shape=jax.ShapeDtypeStruct((T, MLA_HEADS * MLA_V), BF16),
        scratch_shapes=[pltpu.VMEM((2, tq, LANES), F32), pltpu.VMEM((2, tq, LANES), F32),
                        pltpu.VMEM((2, tq, 2 * MLA_V), F32)],
        compiler_params=_cparams(3), name="mla_attn",
    )(q, k, v)


_XI_F, _ZETA_F, _XI_B, _ZETA_B, _CARRY_F, _CARRY_B = range(6)


def _ret_rope(r_ref, cos, sin):
    r = r_ref[...].astype(F32)
    tiles = [r[:, i * LANES:(i + 1) * LANES] for i in range(r.shape[1] // LANES)]
    return jnp.concatenate([t * cos + _rot_half(t, None, RET_DK // 2) * sin for t in tiles], axis=1)


def _head_masked(x_pair, h):
    lane = lax.broadcasted_iota(jnp.int32, x_pair.shape, 1)
    lo = (h % 2) * RET_DK
    return jnp.where((lane >= lo) & (lane < lo + RET_DK), x_pair, 0.0)


def _ret_fwd_body(rq_ref, rk_ref, rv_ref, cos_ref, sin_ref, dm_ref, vec_ref, y_ref, s_ref):
    @pl.when(pl.program_id(1) == 0)
    def _():
        s_ref[...] = jnp.zeros(s_ref.shape, F32)

    q = _ret_rope(rq_ref, cos_ref[...], sin_ref[...])
    k = _ret_rope(rk_ref, cos_ref[...], sin_ref[...])
    for h in range(RET_HEADS):
        pair = slice((h // 2) * LANES, (h // 2 + 1) * LANES)
        vcols = slice(h * RET_DV, (h + 1) * RET_DV)
        qm = _head_masked(q[:, pair], h)
        kp = k[:, pair]
        vh = rv_ref[:, vcols]
        scores = _dot_nt(qm.astype(BF16), kp.astype(BF16)) * dm_ref[h]
        state = s_ref[h]
        y_ref[:, vcols] = (_dot(scores.astype(BF16), vh)
                           + _dot((qm * vec_ref[h, _XI_F]).astype(BF16), state.astype(BF16)))
        kz = (kp * vec_ref[h, _ZETA_F]).astype(BF16)
        s_ref[h] = vec_ref[h, _CARRY_F, 0:LANES, :] * state + _dot(_transposed(kz), vh)


def _ret_bwd_body(rq_ref, rk_ref, rv_ref, rg_ref, ya_ref, cos_ref, sin_ref, vec_ref, o_ref, s_ref):
    @pl.when(pl.program_id(1) == 0)
    def _():
        s_ref[...] = jnp.zeros(s_ref.shape, F32)

    q = _ret_rope(rq_ref, cos_ref[...], sin_ref[...])
    k = _ret_rope(rk_ref, cos_ref[...], sin_ref[...])
    for h in range(RET_HEADS):
        pair = slice((h // 2) * LANES, (h // 2 + 1) * LANES)
        vcols = slice(h * RET_DV, (h + 1) * RET_DV)
        qm = _head_masked(q[:, pair], h)
        state = s_ref[h]
        y = ya_ref[:, vcols] + _dot((qm * vec_ref[h, _XI_B]).astype(BF16), state.astype(BF16))
        kz = (k[:, pair] * vec_ref[h, _ZETA_B]).astype(BF16)
        s_ref[h] = vec_ref[h, _CARRY_B, 0:LANES, :] * state + _dot(_transposed(kz), rv_ref[:, vcols])
        mu = jnp.mean(y, axis=-1, keepdims=True)
        yc = y - mu
        var = jnp.mean(yc * yc, axis=-1, keepdims=True)
        yn = yc * lax.rsqrt(var + GN_EPS)
        gate = jax.nn.silu(rg_ref[:, vcols].astype(F32))
        o_ref[:, vcols] = (gate * yn).astype(o_ref.dtype)


def _retention(rq, rk, rv, rg, cos_r, sin_r, l, dmask, vec, B, S, tr):
    T = rq.shape[0]
    nb = S // tr
    w = RET_HEADS * RET_DV
    fwd = lambda b, j: (b * nb + j, 0)
    bwd = lambda b, j: (b * nb + nb - 1 - j, 0)
    state = pltpu.VMEM((RET_HEADS, LANES, RET_DV), F32)
    ya = pl.pallas_call(
        _ret_fwd_body, grid=(B, nb),
        in_specs=[pl.BlockSpec((tr, rq.shape[1]), fwd), pl.BlockSpec((tr, rk.shape[1]), fwd),
                  pl.BlockSpec((tr, w), fwd), pl.BlockSpec((tr, LANES), fwd),
                  pl.BlockSpec((tr, LANES), fwd), _layer_spec(dmask, l), _layer_spec(vec, l)],
        out_specs=pl.BlockSpec((tr, w), fwd),
        out_shape=jax.ShapeDtypeStruct((T, w), F32),
        scratch_shapes=[state], compiler_params=_cparams(2), name="ret_fwd",
    )(rq, rk, rv, cos_r, sin_r, dmask, vec)
    return pl.pallas_call(
        _ret_bwd_body, grid=(B, nb),
        in_specs=[pl.BlockSpec((tr, rq.shape[1]), bwd), pl.BlockSpec((tr, rk.shape[1]), bwd),
                  pl.BlockSpec((tr, w), bwd), pl.BlockSpec((tr, w), bwd), pl.BlockSpec((tr, w), bwd),
                  pl.BlockSpec((tr, LANES), bwd), pl.BlockSpec((tr, LANES), bwd),
                  _layer_spec(vec, l)],
        out_specs=pl.BlockSpec((tr, w), bwd),
        out_shape=jax.ShapeDtypeStruct((T, w), BF16),
        scratch_shapes=[state], compiler_params=_cparams(2), name="ret_bwd",
    )(rq, rk, rv, rg, ya, cos_r, sin_r, vec)


PAIR_W = 2 * S5_GROUP_CH
PAIRS_PER_TILE = LANES // PAIR_W
CHUNK_W = S5_CHUNK * PAIR_W


def _chunk_rows(u_ref, n):
    tok = [u_ref[pl.ds(i, n, stride=S5_CHUNK), :] for i in range(S5_CHUNK)]
    return [jnp.concatenate([t[:, p * PAIR_W:(p + 1) * PAIR_W] for t in tok], axis=1)
            for p in range(PAIRS_PER_TILE)]


def _s5_in_body(u_ref, min_ref, fr_ref, fi_ref, br_ref, bi_ref):
    n = fr_ref.shape[1]
    for p, rows in enumerate(_chunk_rows(u_ref, n)):
        v = _dot_nt(rows.astype(BF16), min_ref[p])
        for part, ref in enumerate((fr_ref, fi_ref, br_ref, bi_ref)):
            ref[0, :, p * LANES:(p + 1) * LANES] = v[:, part * LANES:(part + 1) * LANES]


def _s5_scan_body(fr_ref, fi_ref, br_ref, bi_ref, a_ref, zfr_ref, zfi_ref, zbr_ref, zbi_ref, *, rb):
    n = fr_ref.shape[1]
    nblk = n // rb
    lw = fr_ref.shape[2]
    afr, afi = a_ref[0:1, :], a_ref[1:2, :]
    abr, abi = a_ref[2:3, :], a_ref[3:4, :]
    zero = jnp.zeros((1, lw), F32)

    def step(blk, carry):
        xr, xi, yr, yi = carry
        r0 = pl.multiple_of(blk * rb, rb)
        vr, vi = fr_ref[0, pl.ds(r0, rb), :], fi_ref[0, pl.ds(r0, rb), :]
        out_r, out_i = [], []
        for r in range(rb):
            out_r.append(xr)
            out_i.append(xi)
            xr, xi = (afr * xr - afi * xi + vr[r:r + 1], afr * xi + afi * xr + vi[r:r + 1])
        zfr_ref[0, pl.ds(r0, rb), :] = jnp.concatenate(out_r, axis=0).astype(zfr_ref.dtype)
        zfi_ref[0, pl.ds(r0, rb), :] = jnp.concatenate(out_i, axis=0).astype(zfi_ref.dtype)
        r1 = pl.multiple_of((nblk - 1 - blk) * rb, rb)
        wr, wi = br_ref[0, pl.ds(r1, rb), :], bi_ref[0, pl.ds(r1, rb), :]
        out_r, out_i = [None] * rb, [None] * rb
        for r in reversed(range(rb)):
            out_r[r] = yr
            out_i[r] = yi
            yr, yi = (abr * yr - abi * yi + wr[r:r + 1], abr * yi + abi * yr + wi[r:r + 1])
        zbr_ref[0, pl.ds(r1, rb), :] = jnp.concatenate(out_r, axis=0).astype(zbr_ref.dtype)
        zbi_ref[0, pl.ds(r1, rb), :] = jnp.concatenate(out_i, axis=0).astype(zbi_ref.dtype)
        return xr, xi, yr, yi

    lax.fori_loop(0, nblk, step, (zero, zero, zero, zero))


def _s5_out_body(u_ref, zfr_ref, zfi_ref, zbr_ref, zbi_ref, t_ref, mout_ref, y_ref):
    n = zfr_ref.shape[1]
    ys = []
    for p, rows in enumerate(_chunk_rows(u_ref, n)):
        cols = slice(p * LANES, (p + 1) * LANES)
        z = jnp.concatenate([r[0, :, cols] for r in (zfr_ref, zfi_ref, zbr_ref, zbi_ref)], axis=1)
        ys.append(_dot(rows.astype(BF16), t_ref[p]) + _dot(z, mout_ref[p]))
    for i in range(S5_CHUNK):
        y_ref[pl.ds(i, n, stride=S5_CHUNK), :] = jnp.concatenate(
            [y[:, i * PAIR_W:(i + 1) * PAIR_W] for y in ys], axis=1)


def _s5_core(u, l, m_in, t_mat, m_out, a_vec, B, S):
    N = S // S5_CHUNK
    n_tiles = S5_WIDTH // LANES
    sw = S5_GROUPS * S5_STATE
    pw = PAIRS_PER_TILE * LANES
    u_spec = pl.BlockSpec((S, LANES), lambda b, q: (b, q))
    part_spec = pl.BlockSpec((1, N, pw), lambda b, q: (b, 0, q))
    mat_spec = pl.BlockSpec((None, PAIRS_PER_TILE, CHUNK_W, CHUNK_W), lambda b, q: (l, q, 0, 0))
    parts = pl.pallas_call(
        _s5_in_body, grid=(B, n_tiles),
        in_specs=[u_spec, mat_spec], out_specs=[part_spec] * 4,
        out_shape=[jax.ShapeDtypeStruct((B, N, sw), F32)] * 4,
        compiler_params=_cparams(2), name="s5_in",
    )(u, m_in)
    lw = min(S5_SCAN_LANES, sw)
    blk = pl.BlockSpec((1, N, lw), lambda b, j: (b, 0, j))
    z = pl.pallas_call(
        functools.partial(_s5_scan_body, rb=16), grid=(B, sw // lw),
        in_specs=[blk] * 4 + [pl.BlockSpec((None, 4, lw), lambda b, j: (l, 0, j))],
        out_specs=[blk] * 4, out_shape=[jax.ShapeDtypeStruct((B, N, sw), BF16)] * 4,
        compiler_params=_cparams(2), name="s5_scan",
    )(*parts, a_vec)
    return pl.pallas_call(
        _s5_out_body, grid=(B, n_tiles),
        in_specs=[u_spec] + [part_spec] * 4 + [mat_spec, mat_spec],
        out_specs=u_spec, out_shape=jax.ShapeDtypeStruct((B * S, S5_WIDTH), F32),
        compiler_params=_cparams(2), name="s5_out",
    )(u, *z, t_mat, m_out)


def _merge_body(x_ref, mod_ref, om_ref, or_ref, ys_ref, u_ref, g_ref, d_ref, wglu_ref,
                wm_ref, wr_ref, ws_ref, wo_ref, lng_ref, lnb_ref, o_ref, *, alpha):
    D = x_ref.shape[1]
    y = d_ref[...] * u_ref[...].astype(F32) + ys_ref[...].astype(F32)
    y = jax.nn.gelu(y)
    o_s5 = y * jax.nn.sigmoid(_dot(y.astype(BF16), wglu_ref[...]))
    merged = (jax.nn.sigmoid(g_ref[:, 0:D].astype(F32)) * _dot(om_ref[...], wm_ref[...])
              + jax.nn.sigmoid(g_ref[:, D:2 * D].astype(F32)) * _dot(or_ref[...], wr_ref[...])
              + jax.nn.sigmoid(g_ref[:, 2 * D:3 * D].astype(F32)) * _dot(o_s5.astype(BF16), ws_ref[...]))
    out = _dot(merged.astype(BF16), wo_ref[...])
    gate1 = mod_ref[0, :, 2 * D:3 * D]
    o_ref[...] = _layer_norm(alpha * x_ref[...] + gate1 * out) * lng_ref[...] + lnb_ref[...]


def _merge(x, l, mod, o_mla, o_ret, y_ssm, u, gates, d_skip, w_glu, wm, wr, ws, wo, ln_g, ln_b,
           tm, tiles_per_seq, alpha):
    T, D = x.shape
    row = lambda i: (i, 0)
    consts = (d_skip, w_glu, wm, wr, ws, wo, ln_g, ln_b)
    return pl.pallas_call(
        functools.partial(_merge_body, alpha=alpha), grid=(T // tm,),
        in_specs=[pl.BlockSpec((tm, D), row), _mod_spec(mod, l, tiles_per_seq)]
                 + [pl.BlockSpec((tm, a.shape[1]), row) for a in (o_mla, o_ret, y_ssm, u, gates)]
                 + [_layer_spec(a, l) for a in consts],
        out_specs=pl.BlockSpec((tm, D), row),
        out_shape=jax.ShapeDtypeStruct((T, D), F32),
        compiler_params=_cparams(1), name="merge",
    )(x, mod, o_mla, o_ret, y_ssm, u, gates, *consts)


HALO = 8


def _ffn_body(x_ref, xp_ref, xn_ref, mod_ref, wup_ref, cw_ref, wd_ref,
              lng_ref, lnb_ref, o_ref, act_ref, *, alpha, tiles_per_seq):
    tm, D = x_ref.shape
    dff = wd_ref.shape[0]
    i = pl.program_id(0)
    shift = mod_ref[0, :, 3 * D:4 * D]
    scale = mod_ref[0, :, 4 * D:5 * D]
    gate2 = mod_ref[0, :, 5 * D:6 * D]
    x = x_ref[...]
    x_ext = jnp.concatenate([xp_ref[...], x, xn_ref[...]], axis=0)
    h = _layer_norm(x_ext) * (1.0 + scale) + shift
    r = lax.broadcasted_iota(jnp.int32, (tm + 2 * HALO, 1), 0)
    pos_in_seq = i % tiles_per_seq
    lo = jnp.where(pos_in_seq == 0, HALO, 0)
    hi = jnp.where(pos_in_seq == tiles_per_seq - 1, tm + HALO, tm + 2 * HALO)
    h = jnp.where((r >= lo) & (r < hi), h, 0.0).astype(BF16)
    n_ext = tm + 2 * HALO

    def conv_up(c0):
        cols = slice(c0, c0 + FFN_CHUNK)
        up = _dot(h, wup_ref[:, cols])
        prev = pltpu.roll(up, 1, 0)
        nxt = pltpu.roll(up, n_ext - 1, 0)
        y = (prev * cw_ref[0:1, cols] + up * cw_ref[1:2, cols] + nxt * cw_ref[2:3, cols]
             + cw_ref[3:4, cols])
        return y[HALO:HALO + tm, :]

    for c0 in range(0, dff, FFN_CHUNK):
        act_ref[:, c0:c0 + FFN_CHUNK] = (jax.nn.silu(conv_up(dff + c0)) * conv_up(c0)).astype(BF16)
    f = _dot(act_ref[...], wd_ref[...])
    o_ref[...] = _layer_norm(alpha * x + gate2 * f) * lng_ref[...] + lnb_ref[...]


def _ffn(x, l, mod, w_up, cw, wd, ln_g, ln_b, tm, tiles_per_seq, alpha):
    T, D = x.shape
    hb = tm // HALO
    n_hb = T // HALO
    consts = (w_up, cw, wd, ln_g, ln_b)
    return pl.pallas_call(
        functools.partial(_ffn_body, alpha=alpha, tiles_per_seq=tiles_per_seq), grid=(T // tm,),
        in_specs=[pl.BlockSpec((tm, D), lambda i: (i, 0)),
                  pl.BlockSpec((HALO, D), lambda i: (jnp.maximum(i * hb - 1, 0), 0)),
                  pl.BlockSpec((HALO, D), lambda i: (jnp.minimum((i + 1) * hb, n_hb - 1), 0)),
                  _mod_spec(mod, l, tiles_per_seq)]
                 + [_layer_spec(a, l) for a in consts],
        out_specs=pl.BlockSpec((tm, D), lambda i: (i, 0)),
        out_shape=jax.ShapeDtypeStruct((T, D), F32),
        scratch_shapes=[pltpu.VMEM((tm, wd.shape[1]), BF16)],
        compiler_params=_cparams(1), name="ffn",
    )(x, x, x, mod, *consts)


def _pack_w_in(w_in):
    widths = [MLA_Q_RANK, MLA_KV_RANK, MLA_ROPE, RET_HEADS * RET_DK, RET_HEADS * RET_DK,
              RET_HEADS * RET_DV, RET_HEADS * RET_DV, S5_WIDTH]
    sp = [0] + [int(v) for v in np.cumsum(widths)]
    w_qc, w_kvc, w_kr, w_rq, w_rk, w_rv, w_rg, w_u = [w_in[..., sp[i]:sp[i + 1]] for i in range(8)]
    w_gates = w_in[..., sp[8]:]
    w_kr = jnp.pad(w_kr, ((0, 0), (0, 0), (MLA_NOPE, HEAD_PAD - MLA_NOPE - MLA_ROPE)))
    blocks = [w_qc, w_kvc, w_kr, w_rq * (RET_DK ** -0.5), w_rk, w_rv, w_rg, w_u, w_gates]
    return jnp.concatenate(blocks, axis=-1).astype(BF16)


def _pack_mla(w_uq, w_ukv):
    L = w_uq.shape[0]
    dq = MLA_NOPE + MLA_ROPE
    wq = w_uq.reshape(L, MLA_Q_RANK, MLA_HEADS, dq)
    wq = jnp.pad(wq, ((0, 0), (0, 0), (0, 0), (0, HEAD_PAD - dq)))
    wkv = w_ukv.reshape(L, MLA_KV_RANK, MLA_HEADS, MLA_NOPE + MLA_V)
    wk = jnp.pad(wkv[..., :MLA_NOPE], ((0, 0), (0, 0), (0, 0), (0, HEAD_PAD - MLA_NOPE)))
    wv = wkv[..., MLA_NOPE:].reshape(L, MLA_KV_RANK, MLA_HEADS * MLA_V)
    flat = lambda w: w.reshape(L, w.shape[1], MLA_HEADS * HEAD_PAD).astype(BF16)
    return flat(wq), flat(wk), wv.astype(BF16)


def _ret_tables(ret_log_decay, c):
    lg = jnp.log1p(-jnp.exp(ret_log_decay.astype(F32)))
    lg_f, lg_b = lg[:, 0, :, None, None], lg[:, 1, :, None, None]
    idx = jnp.arange(c, dtype=F32)
    diff = idx[:, None] - idx[None, :]
    dmask = jnp.where(diff >= 0, jnp.exp(lg_f * jnp.maximum(diff, 0.0)),
                      jnp.exp(lg_b * jnp.maximum(-diff, 0.0)))
    lf, lb = lg[:, 0, :, None], lg[:, 1, :, None]
    ones = jnp.ones((c,), F32)
    cols = [jnp.exp(lf * (idx + 1)), jnp.exp(lf * (c - 1 - idx)), jnp.exp(lb * (c - idx)),
            jnp.exp(lb * idx), jnp.exp(lf * c) * ones, jnp.exp(lb * c) * ones]
    vec = jnp.stack(cols, axis=2)
    return dmask, jnp.broadcast_to(vec[..., None], vec.shape + (LANES,))


def _s5_tables(lam_re, lam_im, log_step, b_re, b_im, c_re, c_im):
    L = lam_re.shape[0]
    G, P, CH, C = S5_GROUPS, S5_STATE, S5_GROUP_CH, S5_CHUNK
    f32 = lambda t: t.astype(F32)
    lam_re, lam_im, b_re, b_im, c_re, c_im = map(f32, (lam_re, lam_im, b_re, b_im, c_re, c_im))
    step = jnp.exp(f32(log_step))[..., None]
    mag = jnp.exp(lam_re * step)
    a_re, a_im = mag * jnp.cos(lam_im * step), mag * jnp.sin(lam_im * step)
    den = jnp.square(lam_re) + jnp.square(lam_im)
    f_re = ((a_re - 1.0) * lam_re + a_im * lam_im) / den
    f_im = (a_im * lam_re - (a_re - 1.0) * lam_im) / den
    bb_re = f_re[..., None] * b_re - f_im[..., None] * b_im
    bb_im = f_re[..., None] * b_im + f_im[..., None] * b_re
    k = jnp.arange(C + 1, dtype=F32)[:, None, None, None, None]
    pm = jnp.exp(k * (lam_re * step))
    pw_re, pw_im = pm * jnp.cos(k * (lam_im * step)), pm * jnp.sin(k * (lam_im * step))

    NP = S5_PAIRS
    lane = np.arange(CHUNK_W)
    tok, grp, ch = lane // PAIR_W, (lane // CH) % 2, lane % CH
    onehot = lambda idx, n: jnp.asarray(np.arange(n)[:, None] == idx[None, :], F32)
    spread = lambda t, sel: jnp.einsum('...k,kn->...n', t, sel, precision=HI)
    ch_sel = onehot(ch, CH)
    own_lanes = jnp.asarray(grp[None, :] == (np.arange(G) % 2)[:, None], F32)[:, None, :]

    def times(z, mat_re, mat_im, expo):
        sel = onehot(expo, C + 1)
        pr = spread(jnp.moveaxis(pw_re[:, :, z], 0, -1), sel) * own_lanes
        pi = spread(jnp.moveaxis(pw_im[:, :, z], 0, -1), sel) * own_lanes
        mr, mi = spread(mat_re[:, z], ch_sel), spread(mat_im[:, z], ch_sel)
        return mr * pr - mi * pi, mr * pi + mi * pr

    cr, ci = jnp.swapaxes(c_re, -1, -2), jnp.swapaxes(c_im, -1, -2)
    pairs = lambda t: t.reshape((L, NP, 2) + t.shape[2:])

    of_re, of_im = times(0, cr, ci, tok + 1)
    ob_re, ob_im = times(1, cr, ci, C - tok)
    m_out = jnp.stack([pairs(t) for t in (of_re, -of_im, ob_re, -ob_im)], axis=2)
    m_out = m_out.reshape(L, NP, 4 * 2 * P, CHUNK_W)
    if_re, if_im = times(0, bb_re, bb_im, C - 1 - tok)
    ib_re, ib_im = times(1, bb_re, bb_im, tok)
    m_in_t = jnp.stack([pairs(t) for t in (if_re, if_im, ib_re, ib_im)], axis=2)
    m_in_t = m_in_t.reshape(L, NP, 4 * 2 * P, CHUNK_W)

    def lag_row(z, expo):
        h_re, h_im = times(z, cr, ci, expo)
        bt_re, bt_im = jnp.swapaxes(bb_re[:, z], -1, -2), jnp.swapaxes(bb_im[:, z], -1, -2)
        return (jnp.einsum('lgcp,lgpn->lgcn', bt_re, h_re, precision=HI)
                - jnp.einsum('lgcp,lgpn->lgcn', bt_im, h_im, precision=HI))

    row_f, row_b = lag_row(0, tok), lag_row(1, C - 1 - tok)
    strip_f = jnp.concatenate([jnp.zeros_like(row_f), row_f], axis=-1)
    strip_b = jnp.concatenate([row_b, jnp.zeros_like(row_b)], axis=-1)
    blocks = [pairs(lax.slice_in_dim(strip_f, CHUNK_W - j * PAIR_W, 2 * CHUNK_W - j * PAIR_W, axis=-1)
                    + lax.slice_in_dim(strip_b, (C - 1 - j) * PAIR_W, (C - 1 - j) * PAIR_W + CHUNK_W, axis=-1))
              for j in range(C)]
    t_p = jnp.stack(blocks, axis=2).reshape(L, NP, CHUNK_W, CHUNK_W)

    a_vec = jnp.stack([pw_re[C, :, 0], pw_im[C, :, 0], pw_re[C, :, 1], pw_im[C, :, 1]], axis=1)
    a_vec = a_vec.reshape(L, 4, G * P)
    return t_p.astype(BF16), m_in_t.astype(BF16), m_out.astype(BF16), a_vec


def _pack_ffn(w_up, conv_w, conv_b, w_down):
    cw = jnp.concatenate([conv_w, conv_b[:, None, :]], axis=1)
    return w_up.astype(BF16), cw, w_down.astype(BF16)


def _tile(n, pref):
    t = min(pref, n)
    assert n % t == 0, (n, t)
    return t


class _Tiles(NamedTuple):
    rows: int
    attn_q: int
    attn_k: int
    rope_rows: int


def _tiles(S, T):
    return _Tiles(rows=_tile(S, 512), attn_q=_tile(S, 2048), attn_k=_tile(S, 1024),
                  rope_rows=_tile(T, 2048))


def kernel(x, c, positions, w_in, mla_q_norm, mla_w_uq, mla_kv_norm, mla_w_ukv, ret_log_decay, s5_lam_re, s5_lam_im, s5_log_step, s5_b_re, s5_b_im, s5_c_re, s5_c_im, s5_d, s5_w_glu, w_branch_mla, w_branch_ret, w_branch_s5, w_o, ffn_w_up, ffn_conv_w, ffn_conv_b, ffn_w_down, ln1_g, ln1_b, ln2_g, ln2_b, w_ada, b_ada):
    B, S, D = x.shape
    L = w_in.shape[0]
    T = B * S
    assert S % LANES == 0 and S % (16 * S5_CHUNK) == 0 and ffn_w_up.shape[2] % (2 * FFN_CHUNK) == 0
    alpha = (2 * L) ** 0.25
    tiles = _tiles(S, T)
    tm = tiles.rows
    tps = S // tm

    cos_m, sin_m, cos_r, sin_r = _rope_tables(positions, tiles.rope_rows)
    mod = _ada_mod(c, w_ada, b_ada).reshape(L, B, 1, -1)

    w_packed = _pack_w_in(w_in)
    wq, wk, wv = _pack_mla(mla_w_uq, mla_w_ukv)
    dmask, ret_vec = _ret_tables(ret_log_decay, tm)
    t_mat, m_in, m_out, a_vec = _s5_tables(s5_lam_re, s5_lam_im, s5_log_step, s5_b_re, s5_b_im,
                                           s5_c_re, s5_c_im)
    w_up, conv_cw, w_down = _pack_ffn(ffn_w_up, ffn_conv_w, ffn_conv_b, ffn_w_down)
    bf = lambda w: w.astype(BF16)
    row2 = lambda v: v.reshape(L, 1, -1)
    q_gain, kv_gain = row2(mla_q_norm), row2(mla_kv_norm)
    d_skip = row2(s5_d)
    g1, b1, g2, b2 = row2(ln1_g), row2(ln1_b), row2(ln2_g), row2(ln2_b)
    w_glu, wbm, wbr, wbs, wo = map(bf, (s5_w_glu, w_branch_mla, w_branch_ret, w_branch_s5, w_o))
    xf = x.reshape(T, D)
    for l in range(L):
        q, k, v, rq, rk, rv, rg, u, gates = _in_proj(xf, l, mod, w_packed, cos_m, sin_m, q_gain, kv_gain,
                                                     wq, wk, wv, tm, tps)
        o_mla = _mla_attn(q, k, v, B, S, tiles.attn_q, tiles.attn_k)
        o_ret = _retention(rq, rk, rv, rg, cos_r, sin_r, l, dmask, ret_vec, B, S, tm)
        y_ssm = _s5_core(u, l, m_in, t_mat, m_out, a_vec, B, S)
        x1 = _merge(xf, l, mod, o_mla, o_ret, y_ssm, u, gates, d_skip, w_glu, wbm, wbr, wbs, wo, g1, b1,
                    tm, tps, alpha)
        xf = _ffn(x1, l, mod, w_up, conv_cw, w_down, g2, b2, tm, tps, alpha)
    return xf.reshape(B, S, D)
```
